```python
import math
import jax, jax.numpy as jnp
from jax import lax
import numpy as np

D_MODEL = 2048
BATCH = 2
SEQ = 4096
DEPTH = 4
DEC_BATCH = 8
DEC_SEQ = 1
PAST_LEN = 16384
PAGE_SIZE = 128

N_A_LAYERS = DEPTH // 2
N_B_LAYERS = DEPTH - N_A_LAYERS
HEAD_DIM = 128
MIX_W = D_MODEL
N_MEM_HEADS = 4
MEM_W = N_MEM_HEADS * HEAD_DIM
N_MEM_TOKENS = 256
CONV_CH = MIX_W - MEM_W
CONV_W = 31
N_MOBA_HEADS = (MIX_W - MEM_W) // HEAD_DIM
MOBA_W = N_MOBA_HEADS * HEAD_DIM
MOBA_BLOCK = 256
MOBA_TOPK = 3
MOBA_QCHUNK = 32
ROPE_THETA = 10000.0
N_EXPERTS = 32
TOP_K = 4
D_FF = D_MODEL
SWIGLU_LIMIT = 7.0
SWIGLU_ALPHA = 1.702
MOE_BLOCK_MAX = 512
LN_EPS = 1e-5
DEEPNORM_ALPHA = (2 * DEPTH) ** 0.25
DEEPNORM_BETA = (8 * DEPTH) ** -0.25
NEG_INF = -1e30

kernel_name = 'yoco_conformer_moba_moe_decoder_step'


def layer_norm(x, g, b):
    xf = x.astype(jnp.float32)
    mu = xf.mean(-1, keepdims=True)
    var = jnp.square(xf - mu).mean(-1, keepdims=True)
    return ((xf - mu) * lax.rsqrt(var + LN_EPS) * g.astype(jnp.float32) + b.astype(jnp.float32)).astype(x.dtype)


def rope(x, pos):
    half = HEAD_DIM // 2
    inv = ROPE_THETA ** (-jnp.arange(half, dtype=jnp.float32) / half)
    ang = pos.astype(jnp.float32)[:, None] * inv[None, :]
    cos = jnp.cos(ang)[None, :, None, :]
    sin = jnp.sin(ang)[None, :, None, :]
    xf = x.astype(jnp.float32)
    x1, x2 = xf[..., :half], xf[..., half:]
    return jnp.concatenate([x1 * cos - x2 * sin, x2 * cos + x1 * sin], -1).astype(x.dtype)


def mem_attention(q, mk, mv):
    s = jnp.einsum('blhd,bmhd->bhlm', q.astype(jnp.float32), mk.astype(jnp.float32)) * (HEAD_DIM ** -0.5)
    p = jax.nn.softmax(s, axis=-1)
    return jnp.einsum('bhlm,bmhd->blhd', p, mv.astype(jnp.float32)).astype(q.dtype)


def depthwise_conv(u_full, w, b):
    c = u_full.shape[-1]
    y = lax.conv_general_dilated(u_full, w[:, None, :].astype(u_full.dtype), (1,), 'VALID',
                                 dimension_numbers=('NWC', 'WIO', 'NWC'), feature_group_count=c)
    return y + b.astype(y.dtype)


def moe_ffn(x, w_router, b_router, w_gu, b_gu, w_down, b_down):
    bsz, L, d = x.shape
    t = bsz * L
    xt = x.reshape(t, d)
    logits = xt.astype(jnp.float32) @ w_router.astype(jnp.float32) + b_router.astype(jnp.float32)
    top_v, top_e = lax.top_k(logits, TOP_K)
    gates = jax.nn.softmax(top_v, axis=-1)
    tk = t * TOP_K
    per_e = -(-tk // N_EXPERTS)
    blk = 8
    while blk < per_e and blk < MOE_BLOCK_MAX:
        blk *= 2
    n_rows = -(-(tk + N_EXPERTS * (blk - 1)) // blk) * blk
    n_blk = n_rows // blk
    flat_e = top_e.reshape(-1).astype(jnp.int32)
    flat_t = jnp.repeat(jnp.arange(t, dtype=jnp.int32), TOP_K)
    flat_g = gates.reshape(-1)
    order = jnp.argsort(flat_e)
    se = flat_e[order]
    counts = jnp.bincount(flat_e, length=N_EXPERTS)
    pcounts = (counts + blk - 1) // blk * blk
    pends = jnp.cumsum(pcounts)
    pstarts = pends - pcounts
    starts = jnp.cumsum(counts) - counts
    dest = pstarts[se] + (jnp.arange(tk, dtype=jnp.int32) - starts[se])
    row_tok = jnp.zeros((n_rows,), jnp.int32).at[dest].set(flat_t[order])
    row_gate = jnp.zeros((n_rows,), jnp.float32).at[dest].set(flat_g[order])
    blk_e = jnp.clip(jnp.searchsorted(pends, jnp.arange(n_blk) * blk, side='right'), 0, N_EXPERTS - 1)
    xr = xt[row_tok].reshape(n_blk, blk, d)

    def expert_block(args):
        xb, e = args
        gu = xb @ w_gu[e] + b_gu[e]
        glu = jnp.minimum(gu[:, :D_FF], SWIGLU_LIMIT)
        lin = jnp.clip(gu[:, D_FF:], -SWIGLU_LIMIT, SWIGLU_LIMIT)
        h = glu * jax.nn.sigmoid(SWIGLU_ALPHA * glu) * (lin + 1)
        return h @ w_down[e] + b_down[e]

    yr = lax.map(expert_block, (xr, blk_e)).reshape(n_rows, d)
    y = jnp.zeros((t, d), x.dtype).at[row_tok].add((yr * row_gate[:, None].astype(yr.dtype)).astype(x.dtype))
    return y.reshape(bsz, L, d)


def moba_blocks(k_all, v_all):
    bsz, L, h, hd = k_all.shape
    l_pad = -(-L // MOBA_BLOCK) * MOBA_BLOCK
    pad = ((0, 0), (0, l_pad - L), (0, 0), (0, 0))
    nb = l_pad // MOBA_BLOCK
    kb = jnp.pad(k_all, pad).reshape(bsz, nb, MOBA_BLOCK, h, hd).transpose(0, 3, 1, 2, 4)
    vb = jnp.pad(v_all, pad).reshape(bsz, nb, MOBA_BLOCK, h, hd).transpose(0, 3, 1, 2, 4)
    kmean = kb.astype(jnp.float32).mean(axis=3)
    return kb, vb, kmean


def moba_query_block(q, pos, kb, vb, kmean):
    bsz, nq, h, hd = q.shape
    nb = kb.shape[2]
    k_sel = min(MOBA_TOPK, nb)
    scale = HEAD_DIM ** -0.5
    qf = q.astype(jnp.float32)
    own = pos[0] // MOBA_BLOCK
    gate = jnp.einsum('bqhd,bhnd->bhqn', qf, kmean)
    gate = jnp.where(jnp.arange(nb) < own, gate, NEG_INF)
    _, sel = lax.top_k(gate, k_sel)
    sel_ok = sel < own
    bi = jnp.arange(bsz)[:, None, None, None]
    hi = jnp.arange(h)[None, :, None, None]
    k_g = kb[bi, hi, sel].astype(jnp.float32)
    v_g = vb[bi, hi, sel].astype(jnp.float32)
    k_o = lax.dynamic_index_in_dim(kb, own, axis=2, keepdims=False).astype(jnp.float32)
    v_o = lax.dynamic_index_in_dim(vb, own, axis=2, keepdims=False).astype(jnp.float32)
    s_g = jnp.einsum('bqhd,bhqjkd->bhqjk', qf, k_g) * scale
    s_g = jnp.where(sel_ok[..., None], s_g, NEG_INF).reshape(bsz, h, nq, k_sel * MOBA_BLOCK)
    s_o = jnp.einsum('bqhd,bhkd->bhqk', qf, k_o) * scale
    key_pos = own * MOBA_BLOCK + jnp.arange(MOBA_BLOCK)
    s_o = jnp.where(key_pos[None, :] <= pos[:, None], s_o, NEG_INF)
    p = jax.nn.softmax(jnp.concatenate([s_g, s_o], axis=-1), axis=-1)
    p_g = p[..., :k_sel * MOBA_BLOCK].reshape(bsz, h, nq, k_sel, MOBA_BLOCK)
    p_o = p[..., k_sel * MOBA_BLOCK:]
    o = jnp.einsum('bhqjk,bhqjkd->bqhd', p_g, v_g) + jnp.einsum('bhqk,bhkd->bqhd', p_o, v_o)
    return o.astype(q.dtype)


def moba_attention(q, pos, kb, vb, kmean):
    bsz, L, h, hd = q.shape
    qc = math.gcd(L, MOBA_QCHUNK)
    nc = L // qc
    qs = q.reshape(bsz, nc, qc, h, hd).transpose(1, 0, 2, 3, 4)
    ps = pos.reshape(nc, qc)
    out = lax.map(lambda a: moba_query_block(a[0], a[1], kb, vb, kmean), (qs, ps))
    return out.transpose(1, 0, 2, 3, 4).reshape(bsz, L, h, hd)


def mixer_a(x, conv_prev, mk, mv, w_in, w_dw, b_dw, cg, cb, w_out):
    bsz, L, _ = x.shape
    hcat = x @ w_in
    a, g, qm = jnp.split(hcat, [CONV_CH, 2 * CONV_CH], axis=-1)
    u = a * jax.nn.sigmoid(g)
    u_full = jnp.concatenate([conv_prev.astype(u.dtype), u], axis=1)
    c = jax.nn.silu(layer_norm(depthwise_conv(u_full, w_dw, b_dw), cg, cb))
    m = mem_attention(qm.reshape(bsz, L, N_MEM_HEADS, HEAD_DIM), mk, mv).reshape(bsz, L, MEM_W)
    return jnp.concatenate([c, m], axis=-1) @ w_out, u_full[:, -(CONV_W - 1):]


def mixer_b(x, pos, kb, vb, kmean, mk, mv, w_in, w_out):
    bsz, L, _ = x.shape
    hcat = x @ w_in
    qb = rope(hcat[..., :MOBA_W].reshape(bsz, L, N_MOBA_HEADS, HEAD_DIM), pos)
    o = moba_attention(qb, pos, kb, vb, kmean).reshape(bsz, L, MOBA_W)
    m = mem_attention(hcat[..., MOBA_W:].reshape(bsz, L, N_MEM_HEADS, HEAD_DIM), mk, mv).reshape(bsz, L, MEM_W)
    return jnp.concatenate([o, m], axis=-1) @ w_out


def trunk(x, pos, conv_bufs, mem_k, mem_v, k_past, v_past,
          w_in_a, w_dw, b_dw, conv_ln_g, conv_ln_b, w_out_a, w_in_b, w_out_b, w_kv_shared,
          ln1_g, ln1_b, ln2_g, ln2_b, w_router, b_router, w_gu, b_gu, w_down, b_down):
    bsz, L, _ = x.shape
    new_bufs = []
    k_new = v_new = kb = vb = kmean = None
    for l in range(DEPTH):
        if l < N_A_LAYERS:
            mix, buf = mixer_a(x, conv_bufs[l], mem_k[l], mem_v[l], w_in_a[l], w_dw[l], b_dw[l],
                               conv_ln_g[l], conv_ln_b[l], w_out_a[l])
            new_bufs.append(buf)
        else:
            if l == N_A_LAYERS:
                kv = x @ w_kv_shared
                k_new = rope(kv[..., :MOBA_W].reshape(bsz, L, N_MOBA_HEADS, HEAD_DIM), pos)
                v_new = kv[..., MOBA_W:].reshape(bsz, L, N_MOBA_HEADS, HEAD_DIM)
                if k_past is None:
                    k_all, v_all = k_new, v_new
                else:
                    k_all = jnp.concatenate([k_past.astype(k_new.dtype), k_new], axis=1)
                    v_all = jnp.concatenate([v_past.astype(v_new.dtype), v_new], axis=1)
                kb, vb, kmean = moba_blocks(k_all, v_all)
            j = l - N_A_LAYERS
            mix = mixer_b(x, pos, kb, vb, kmean, mem_k[l], mem_v[l], w_in_b[j], w_out_b[j])
        x = layer_norm(DEEPNORM_ALPHA * x + mix, ln1_g[l], ln1_b[l])
        x = layer_norm(DEEPNORM_ALPHA * x + moe_ffn(x, w_router[l], b_router[l], w_gu[l], b_gu[l],
                                                     w_down[l], b_down[l]), ln2_g[l], ln2_b[l])
    return x, jnp.stack(new_bufs), k_new, v_new


def setup_inputs(seed: int = 0) -> dict:
    key = jax.random.key(seed)
    keys = list(jax.random.split(key, 40))

    def nrm(shape, scale):
        return jax.random.normal(keys.pop(), shape, jnp.float32) * scale

    n_pages = PAST_LEN // PAGE_SIZE
    n_used = DEC_BATCH * n_pages
    n_pool = n_used + max(1, n_used // 4)
    page_table = jax.random.permutation(keys.pop(), n_pool)[:n_used].reshape(DEC_BATCH, n_pages).astype(jnp.int32)
    d = D_MODEL
    return {
        'x_prompt': nrm((BATCH, SEQ, d), 1.0),
        'x_sample': nrm((DEC_BATCH, DEC_SEQ, d), 1.0),
        'cache_conv': nrm((N_A_LAYERS, DEC_BATCH, CONV_W - 1, CONV_CH), 0.5),
        'cache_k': nrm((n_pool, PAGE_SIZE, N_MOBA_HEADS, HEAD_DIM), 1.0),
        'cache_v': nrm((n_pool, PAGE_SIZE, N_MOBA_HEADS, HEAD_DIM), 1.0),
        'cache_mem_k': nrm((DEPTH, DEC_BATCH, N_MEM_TOKENS, N_MEM_HEADS, HEAD_DIM), 1.0),
        'cache_mem_v': nrm((DEPTH, DEC_BATCH, N_MEM_TOKENS, N_MEM_HEADS, HEAD_DIM), 1.0),
        'page_table': page_table,
        'mem_prompt': nrm((BATCH, N_MEM_TOKENS, d), 1.0),
        'w_in_a': nrm((N_A_LAYERS, d, 2 * CONV_CH + MEM_W), d ** -0.5),
        'w_dw': nrm((N_A_LAYERS, CONV_W, CONV_CH), CONV_W ** -0.5),
        'b_dw': nrm((N_A_LAYERS, CONV_CH), 0.01),
        'conv_ln_g': 1.0 + nrm((N_A_LAYERS, CONV_CH), 0.02),
        'conv_ln_b': nrm((N_A_LAYERS, CONV_CH), 0.01),
        'w_out_a': nrm((N_A_LAYERS, CONV_CH + MEM_W, d), (CONV_CH + MEM_W) ** -0.5 * DEEPNORM_BETA),
        'w_in_b': nrm((N_B_LAYERS, d, MOBA_W + MEM_W), d ** -0.5),
        'w_out_b': nrm((N_B_LAYERS, MOBA_W + MEM_W, d), (MOBA_W + MEM_W) ** -0.5 * DEEPNORM_BETA),
        'w_kv_shared': nrm((d, 2 * MOBA_W), d ** -0.5),
        'w_mem_kv': nrm((DEPTH, d, 2 * MEM_W), d ** -0.5),
        'ln1_g': 1.0 + nrm((DEPTH, d), 0.02),
        'ln1_b': nrm((DEPTH, d), 0.01),
        'ln2_g': 1.0 + nrm((DEPTH, d), 0.02),
        'ln2_b': nrm((DEPTH, d), 0.01),
        'w_router': nrm((DEPTH, d, N_EXPERTS), d ** -0.5),
        'b_router': nrm((DEPTH, N_EXPERTS), 0.01),
        'w_gu': nrm((DEPTH, N_EXPERTS, d, 2 * D_FF), d ** -0.5),
        'b_gu': nrm((DEPTH, N_EXPERTS, 2 * D_FF), 0.01),
        'w_down': nrm((DEPTH, N_EXPERTS, D_FF, d), D_FF ** -0.5 * DEEPNORM_BETA),
        'b_down': nrm((DEPTH, N_EXPERTS, d), 0.01),
    }


def reference(x_prompt, x_sample, cache_conv, cache_k, cache_v, cache_mem_k, cache_mem_v, page_table,
              mem_prompt, w_in_a, w_dw, b_dw, conv_ln_g, conv_ln_b, w_out_a, w_in_b, w_out_b, w_kv_shared,
              w_mem_kv, ln1_g, ln1_b, ln2_g, ln2_b, w_router, b_router, w_gu, b_gu, w_down, b_down):
    weights = (w_in_a, w_dw, b_dw, conv_ln_g, conv_ln_b, w_out_a, w_in_b, w_out_b, w_kv_shared,
               ln1_g, ln1_b, ln2_g, ln2_b, w_router, b_router, w_gu, b_gu, w_down, b_down)
    bsz, seq, _ = x_prompt.shape
    n_mem = mem_prompt.shape[1]
    mkv = jnp.einsum('bmd,lde->lbme', mem_prompt, w_mem_kv)
    new_mem_k_prompt = mkv[..., :MEM_W].reshape(DEPTH, bsz, n_mem, N_MEM_HEADS, HEAD_DIM)
    new_mem_v_prompt = mkv[..., MEM_W:].reshape(DEPTH, bsz, n_mem, N_MEM_HEADS, HEAD_DIM)
    conv0 = jnp.zeros((N_A_LAYERS, bsz, CONV_W - 1, CONV_CH), x_prompt.dtype)
    y_prompt, new_conv_prompt, new_k_prompt, new_v_prompt = trunk(
        x_prompt, jnp.arange(seq, dtype=jnp.int32), conv0, new_mem_k_prompt, new_mem_v_prompt, None, None,
        *weights)
    dbsz, dseq, _ = x_sample.shape
    past_len = page_table.shape[1] * cache_k.shape[1]
    k_past = cache_k[page_table].reshape(dbsz, past_len, N_MOBA_HEADS, HEAD_DIM)
    v_past = cache_v[page_table].reshape(dbsz, past_len, N_MOBA_HEADS, HEAD_DIM)
    y_sample, new_conv_sample, new_k_sample, new_v_sample = trunk(
        x_sample, past_len + jnp.arange(dseq, dtype=jnp.int32), cache_conv, cache_mem_k, cache_mem_v,
        k_past, v_past, *weights)
    return (y_prompt, y_sample, new_conv_prompt, new_conv_sample, new_k_prompt, new_v_prompt,
            new_k_sample, new_v_sample, new_mem_k_prompt, new_mem_v_prompt)
```

```python
import functools

import jax
import jax.numpy as jnp
from jax import lax
from jax.experimental import pallas as pl
from jax.experimental.pallas import tpu as pltpu

F32 = jnp.float32
BF16 = jnp.bfloat16

HEAD_DIM = 128
N_MEM_HEADS = 4
MEM_W = N_MEM_HEADS * HEAD_DIM
CONV_W = 31
MOBA_BLOCK = 256
MOBA_TOPK = 3
ROPE_THETA = 10000.0
TOP_K = 4
SWIGLU_LIMIT = 7.0
SWIGLU_ALPHA = 1.702
LN_EPS = 1e-5
NEG_INF = -1e30

ROW_TILE = MOBA_BLOCK
CONV_HALO = 32
SUBLANES = 8
LANES = 128
FF_TILE = 256
MOE_CHUNK = 256
MOE_CAP = 1280
COMBINE_TILE = 128
VMEM_LIMIT = 56 * 1024 * 1024


def _params(n_axes, vmem=VMEM_LIMIT):
    return pltpu.CompilerParams(dimension_semantics=("arbitrary",) * n_axes, vmem_limit_bytes=vmem)


def _resident(shape, index_map):
    return pl.BlockSpec(shape, index_map, pipeline_mode=pl.Buffered(1))


def _layer_norm(z, g, b):
    mu = jnp.mean(z, axis=-1, keepdims=True)
    zc = z - mu
    var = jnp.mean(zc * zc, axis=-1, keepdims=True)
    return zc * lax.rsqrt(var + LN_EPS) * g + b


def _dot_nt(a, b, **kw):
    return lax.dot_general(a, b, (((1,), (1,)), ((), ())), preferred_element_type=F32, **kw)


def _dense_body(x_ref, w_ref, *rest, n_rope_heads, col_chunk):
    if n_rope_heads:
        cos_ref, sin_ref, o_ref = rest
    else:
        (o_ref,) = rest
    x = x_ref[...].astype(BF16)
    n = o_ref.shape[1]
    for c0 in range(0, n, col_chunk):
        cw = min(col_chunk, n - c0)
        y = jnp.dot(x, w_ref[:, c0:c0 + cw], preferred_element_type=F32)
        for h0 in range(0, cw, HEAD_DIM):
            blk = y[:, h0:h0 + HEAD_DIM]
            if (c0 + h0) // HEAD_DIM < n_rope_heads:
                blk = blk * cos_ref[...] + pltpu.roll(blk, HEAD_DIM // 2, 1) * sin_ref[...]
            o_ref[:, c0 + h0:c0 + h0 + HEAD_DIM] = blk


def _dense(x, w, rope=None, n_rope_heads=0, name="dense"):
    m, k = x.shape
    n = w.shape[1]
    tm = min(ROW_TILE, m)
    assert m % tm == 0 and n % HEAD_DIM == 0
    in_specs = [pl.BlockSpec((tm, k), lambda i: (i, 0)), _resident((k, n), lambda i: (0, 0))]
    args = [x, w]
    if n_rope_heads:
        in_specs += [pl.BlockSpec((tm, HEAD_DIM), lambda i: (i, 0))] * 2
        args += list(rope)
    return pl.pallas_call(
        functools.partial(_dense_body, n_rope_heads=n_rope_heads, col_chunk=512),
        grid=(m // tm,),
        in_specs=in_specs,
        out_specs=pl.BlockSpec((tm, n), lambda i: (i, 0)),
        out_shape=jax.ShapeDtypeStruct((m, n), F32),
        compiler_params=_params(1),
        name=name,
    )(*args)


def _conv_body(a_ref, g_ref, prev_ref, w_ref, b_ref, cg_ref, cb_ref, c_ref, tail_ref, us_ref, uext, ybuf,
               *, tiles_per_seq, n_prompt_tiles, db):
    s = pl.program_id(0)
    i = s % tiles_per_seq
    tl, ch = a_ref.shape
    u = a_ref[...] * jax.nn.sigmoid(g_ref[...])

    def ln_swish(y):
        z = _layer_norm(y, cg_ref[...], cb_ref[...])
        return z * jax.nn.sigmoid(z)

    @pl.when(s < n_prompt_tiles)
    def _():
        @pl.when(i == 0)
        def _():
            uext[0:CONV_HALO, :] = jnp.zeros((CONV_HALO, ch), F32)

        uext[CONV_HALO:CONV_HALO + tl, :] = u

        def lane_chunk(cc, carry):
            c0 = pl.multiple_of(cc * LANES, LANES)
            acc = jnp.zeros((tl, LANES), F32) + b_ref[:, pl.ds(c0, LANES)]
            for k in range(CONV_W):
                off = k + CONV_HALO - (CONV_W - 1)
                acc = acc + w_ref[k:k + 1, pl.ds(c0, LANES)] * uext[off:off + tl, pl.ds(c0, LANES)]
            ybuf[:, pl.ds(c0, LANES)] = acc
            return carry

        lax.fori_loop(0, ch // LANES, lane_chunk, 0)
        c_ref[...] = ln_swish(ybuf[...]).astype(BF16)

        @pl.when(i == tiles_per_seq - 1)
        def _():
            tail_ref[0] = uext[tl:tl + CONV_HALO, :]

        uext[0:CONV_HALO, :] = uext[tl:tl + CONV_HALO, :]

    @pl.when(s == n_prompt_tiles)
    def _():
        us = u[0:db, :]
        acc = b_ref[...] + w_ref[CONV_W - 1:CONV_W, :] * us
        for k in range(CONV_W - 1):
            acc = acc + w_ref[k:k + 1, :] * prev_ref[k]
        y = ln_swish(acc)
        c_ref[...] = jnp.concatenate([y, jnp.zeros((tl - db, ch), F32)], axis=0).astype(BF16)
        us_ref[...] = us


def _conv_module(h, prev_t, w_dw, b_dw, cg, cb, *, n_seq, seq_len, ch):
    t_pad = h.shape[0]
    tl = ROW_TILE
    tiles_per_seq = seq_len // tl
    n_prompt_tiles = n_seq * tiles_per_seq
    db = prev_t.shape[1]
    row = lambda v: v.reshape(1, ch)
    return pl.pallas_call(
        functools.partial(_conv_body, tiles_per_seq=tiles_per_seq, n_prompt_tiles=n_prompt_tiles, db=db),
        grid=(n_prompt_tiles + 1,),
        in_specs=[
            pl.BlockSpec((tl, ch), lambda s: (s, 0)),
            pl.BlockSpec((tl, ch), lambda s: (s, 1)),
            _resident((CONV_W - 1, db, ch), lambda s: (0, 0, 0)),
            _resident((CONV_W, ch), lambda s: (0, 0)),
            _resident((1, ch), lambda s: (0, 0)),
            _resident((1, ch), lambda s: (0, 0)),
            _resident((1, ch), lambda s: (0, 0)),
        ],
        out_specs=[
            pl.BlockSpec((tl, ch), lambda s: (s, 0)),
            pl.BlockSpec((1, CONV_HALO, ch), lambda s: (jnp.minimum(s // tiles_per_seq, n_seq - 1), 0, 0)),
            pl.BlockSpec((db, ch), lambda s: (0, 0)),
        ],
        out_shape=[
            jax.ShapeDtypeStruct((t_pad, ch), BF16),
            jax.ShapeDtypeStruct((n_seq, CONV_HALO, ch), F32),
            jax.ShapeDtypeStruct((db, ch), F32),
        ],
        scratch_shapes=[pltpu.VMEM((CONV_HALO + tl, ch), F32), pltpu.VMEM((tl, ch), F32)],
        compiler_params=_params(1),
        name="conv_module",
    )(h, h, prev_t, w_dw, row(b_dw), row(cg), row(cb))


def _attend_full(q, k, v):
    sc = _dot_nt(q, k) * (HEAD_DIM ** -0.5)
    p = jnp.exp(sc - jnp.max(sc, axis=-1, keepdims=True))
    o = jnp.dot(p.astype(BF16), v, preferred_element_type=F32)
    return o / jnp.sum(p, axis=-1, keepdims=True)


def _memattn_body(q_ref, mkp_ref, mvp_ref, mks_ref, mvs_ref, o_ref, *, n_prompt_tiles, db):
    s = pl.program_id(0)
    tl = q_ref.shape[0]
    heads = [slice(h * HEAD_DIM, (h + 1) * HEAD_DIM) for h in range(N_MEM_HEADS)]

    @pl.when(s < n_prompt_tiles)
    def _():
        q = q_ref[...].astype(BF16)
        k = mkp_ref[0].astype(BF16)
        v = mvp_ref[0].astype(BF16)
        for hs in heads:
            o_ref[:, hs] = _attend_full(q[:, hs], k[:, hs], v[:, hs]).astype(BF16)

    @pl.when(s == n_prompt_tiles)
    def _():
        q = q_ref[0:db, :].astype(BF16)
        row = lax.broadcasted_iota(jnp.int32, (db, HEAD_DIM), 0)
        outs = []
        for hs in heads:
            acc = jnp.zeros((db, HEAD_DIM), F32)
            for b in range(db):
                o = _attend_full(q[:, hs], mks_ref[b][:, hs].astype(BF16), mvs_ref[b][:, hs].astype(BF16))
                acc = jnp.where(row == b, o, acc)
            outs.append(acc)
        full = jnp.concatenate(outs, axis=1)
        o_ref[...] = jnp.concatenate([full, jnp.zeros((tl - db, MEM_W), F32)], axis=0).astype(BF16)


def _mem_attention(h, q_col_block, mkp, mvp, mks, mvs, *, n_seq, seq_len):
    t_pad = h.shape[0]
    tl = ROW_TILE
    tiles_per_seq = seq_len // tl
    n_prompt_tiles = n_seq * tiles_per_seq
    db, n_mem = mks.shape[0], mks.shape[1]
    seq_of = lambda s: (jnp.minimum(s // tiles_per_seq, n_seq - 1), 0, 0)
    return pl.pallas_call(
        functools.partial(_memattn_body, n_prompt_tiles=n_prompt_tiles, db=db),
        grid=(n_prompt_tiles + 1,),
        in_specs=[
            pl.BlockSpec((tl, MEM_W), lambda s: (s, q_col_block)),
            pl.BlockSpec((1, n_mem, MEM_W), seq_of),
            pl.BlockSpec((1, n_mem, MEM_W), seq_of),
            _resident((db, n_mem, MEM_W), lambda s: (0, 0, 0)),
            _resident((db, n_mem, MEM_W), lambda s: (0, 0, 0)),
        ],
        out_specs=pl.BlockSpec((tl, MEM_W), lambda s: (s, 0)),
        out_shape=jax.ShapeDtypeStruct((t_pad, MEM_W), BF16),
        compiler_params=_params(1),
        name="mem_attention",
    )(h, mkp, mvp, mks, mvs)


def _outproj_body(*refs, alpha, n_prompt_tiles, has_decode_tile):
    if has_decode_tile:
        a1_ref, a1s_ref, a2_ref, w_ref, x_ref, g_ref, b_ref, o_ref = refs
    else:
        a1_ref, a2_ref, w_ref, x_ref, g_ref, b_ref, o_ref = refs
    s = pl.program_id(0)
    k1 = a1_ref.shape[1]

    def finish(a1):
        y = jnp.dot(a1, w_ref[0:k1, :], preferred_element_type=F32)
        y = y + jnp.dot(a2_ref[...], w_ref[k1:, :], preferred_element_type=F32)
        o_ref[...] = _layer_norm(alpha * x_ref[...] + y, g_ref[...], b_ref[...])

    if has_decode_tile:
        @pl.when(s < n_prompt_tiles)
        def _():
            finish(a1_ref[...])

        @pl.when(s == n_prompt_tiles)
        def _():
            finish(a1s_ref[...])
    else:
        finish(a1_ref[...])


def _outproj_ln(a1, a1s, a2, w, x, g, b, *, alpha):
    t_pad, d = x.shape
    tl = ROW_TILE
    n_prompt_tiles = t_pad // tl - 1
    k1 = a1.shape[1]
    k2 = a2.shape[1]
    has_decode_tile = a1s is not None
    last_a1 = a1.shape[0] // tl - 1
    in_specs = [pl.BlockSpec((tl, k1), lambda s: (jnp.minimum(s, last_a1), 0))]
    args = [a1]
    if has_decode_tile:
        in_specs.append(_resident((tl, k1), lambda s: (0, 0)))
        args.append(a1s)
    in_specs += [
        pl.BlockSpec((tl, k2), lambda s: (s, 0)),
        _resident((k1 + k2, d), lambda s: (0, 0)),
        pl.BlockSpec((tl, d), lambda s: (s, 0)),
        _resident((1, d), lambda s: (0, 0)),
        _resident((1, d), lambda s: (0, 0)),
    ]
    args += [a2, w, x, g.reshape(1, d), b.reshape(1, d)]
    return pl.pallas_call(
        functools.partial(_outproj_body, alpha=alpha, n_prompt_tiles=n_prompt_tiles,
                          has_decode_tile=has_decode_tile),
        grid=(n_prompt_tiles + 1,),
        in_specs=in_specs,
        out_specs=pl.BlockSpec((tl, d), lambda s: (s, 0)),
        out_shape=jax.ShapeDtypeStruct((t_pad, d), F32),
        compiler_params=_params(1),
        name="outproj_ln",
    )(*args)


def _router_body(x_ref, wt_ref, b_ref, e_ref, g_ref):
    logits = _dot_nt(wt_ref[...], x_ref[...], precision=lax.Precision.HIGHEST) + b_ref[...]
    n_e = logits.shape[0]
    eid = lax.broadcasted_iota(jnp.int32, logits.shape, 0).astype(F32)
    vals, idxs = [], []
    for _ in range(TOP_K):
        m = jnp.max(logits, axis=0, keepdims=True)
        idx = jnp.min(jnp.where(logits == m, eid, float(n_e)), axis=0, keepdims=True)
        vals.append(m)
        idxs.append(idx)
        logits = jnp.where(eid == idx, -jnp.inf, logits)
    ex = [jnp.exp(v - vals[0]) for v in vals]
    den = ex[0] + ex[1] + ex[2] + ex[3]
    e_ref[...] = jnp.concatenate(idxs, axis=0).astype(jnp.int32)
    g_ref[...] = jnp.concatenate([e / den for e in ex], axis=0)


def _router(x, w_router_t, b_router):
    t_pad, d = x.shape
    n_e = w_router_t.shape[0]
    tl = ROW_TILE
    return pl.pallas_call(
        _router_body,
        grid=(t_pad // tl,),
        in_specs=[
            pl.BlockSpec((tl, d), lambda i: (i, 0)),
            _resident((n_e, d), lambda i: (0, 0)),
            _resident((n_e, 1), lambda i: (0, 0)),
        ],
        out_specs=[pl.BlockSpec((TOP_K, tl), lambda i: (0, i))] * 2,
        out_shape=[jax.ShapeDtypeStruct((TOP_K, t_pad), jnp.int32), jax.ShapeDtypeStruct((TOP_K, t_pad), F32)],
        compiler_params=_params(1),
        name="router",
    )(x, w_router_t, b_router.reshape(n_e, 1))


def _route_plan(top_e, n_tok, n_e, n_items, t_pad):
    tk = n_tok * TOP_K
    e_flat = top_e[:, :n_tok].T.reshape(tk)
    onehot = (e_flat[:, None] == jnp.arange(n_e, dtype=jnp.int32)[None, :]).astype(jnp.int32)
    csum = jnp.cumsum(onehot, axis=0)
    rank = jnp.take_along_axis(csum, e_flat[:, None], axis=1)[:, 0] - 1
    counts = csum[-1]
    padded = (counts + SUBLANES - 1) // SUBLANES * SUBLANES
    seg_start = jnp.cumsum(padded) - padded
    dest = seg_start[e_flat] + rank
    n_rows = (tk + n_e * (SUBLANES - 1) + SUBLANES - 1) // SUBLANES * SUBLANES
    row_tok = jnp.zeros((n_rows,), jnp.int32).at[dest].set(jnp.arange(tk, dtype=jnp.int32) // TOP_K)
    items_per_e = (counts + MOE_CAP - 1) // MOE_CAP
    item_end = jnp.cumsum(items_per_e)
    total = item_end[-1]
    q = jnp.arange(n_items, dtype=jnp.int32)
    valid = q < total
    e_q = jnp.clip(jnp.searchsorted(item_end, q, side="right"), 0, n_e - 1).astype(jnp.int32)
    e_q = jnp.where(valid, e_q, e_q[jnp.maximum(total - 1, 0)])
    within = q - (item_end - items_per_e)[e_q]
    item_start = jnp.where(valid, seg_start[e_q] + within * MOE_CAP, 0).astype(jnp.int32)
    item_start = jnp.concatenate([item_start, jnp.sum(padded, keepdims=True).astype(jnp.int32)])
    item_n = jnp.where(valid, jnp.clip(counts[e_q] - within * MOE_CAP, 0, MOE_CAP), 0).astype(jnp.int32)
    pos = jnp.zeros((t_pad * TOP_K,), jnp.int32).at[:tk].set(dest.astype(jnp.int32))
    return e_q, item_start, item_n, row_tok, pos, n_rows


def _moe_body(ie_ref, ist_ref, in_ref, tok_ref,
              x_hbm, wg_ref, wu_ref, wd_ref, bg_ref, bu_ref, bd_ref, y_hbm,
              xstage, xb, acc, wgb, wub, wdb, zrows, sems, *, n_items, n_ff):
    q = pl.program_id(0)
    j = pl.program_id(1)
    n = in_ref[q]
    start = ist_ref[q]
    n8 = (n + SUBLANES - 1) // SUBLANES * SUBLANES
    j_issue = min(1, n_ff - 1)

    def gather_copy(r, tok):
        return pltpu.make_async_copy(x_hbm.at[pl.ds(tok, 1), :], xstage.at[pl.ds(r, 1), :], sems.at[0])

    def issue_gather(item):
        base = ist_ref[item]
        cnt = (in_ref[item] + SUBLANES - 1) // SUBLANES * SUBLANES

        def body(r, c):
            gather_copy(r, tok_ref[base + r]).start()
            return c

        lax.fori_loop(0, cnt, body, 0)

    @pl.when((q == 0) & (j == 0))
    def _():
        xstage[...] = jnp.zeros(xstage.shape, F32)
        zrows[...] = jnp.zeros(zrows.shape, F32)
        issue_gather(0)

    @pl.when((j == 0) & (n > 0))
    def _():
        def wait_body(r, c):
            gather_copy(r, 0).wait()
            return c

        lax.fori_loop(0, n8, wait_body, 0)

        def cvt(c, carry):
            r0 = pl.multiple_of(c * MOE_CHUNK, MOE_CHUNK)
            xb[pl.ds(r0, MOE_CHUNK), :] = xstage[pl.ds(r0, MOE_CHUNK), :].astype(BF16)
            return carry

        lax.fori_loop(0, (n + MOE_CHUNK - 1) // MOE_CHUNK, cvt, 0)

    @pl.when((j == j_issue) & (q + 1 < n_items))
    def _():
        issue_gather(jnp.minimum(q + 1, n_items - 1))

    def out_copy(r0):
        return pltpu.make_async_copy(acc.at[pl.ds(pl.multiple_of(r0, SUBLANES), SUBLANES), :],
                                     y_hbm.at[pl.ds(pl.multiple_of(start + r0, SUBLANES), SUBLANES), :],
                                     sems.at[1])

    @pl.when(n > 0)
    def _():
        wgb[...] = wg_ref[...].astype(BF16)
        wub[...] = wu_ref[...].astype(BF16)
        wdb[...] = wd_ref[...].astype(BF16)

        def chunk(c, carry):
            r0 = pl.multiple_of(c * MOE_CHUNK, MOE_CHUNK)
            xc = xb[pl.ds(r0, MOE_CHUNK), :]
            gate = jnp.dot(xc, wgb[...], preferred_element_type=F32) + bg_ref[...]
            lin = jnp.dot(xc, wub[...], preferred_element_type=F32) + bu_ref[...]
            glu = jnp.minimum(gate, SWIGLU_LIMIT)
            lin = jnp.clip(lin, -SWIGLU_LIMIT, SWIGLU_LIMIT)
            hid = glu * jax.nn.sigmoid(SWIGLU_ALPHA * glu) * (lin + 1.0)
            d = jnp.dot(hid.astype(BF16), wdb[...], preferred_element_type=F32)

            @pl.when(j == 0)
            def _():
                acc[pl.ds(r0, MOE_CHUNK), :] = d + bd_ref[...]

            @pl.when(j > 0)
            def _():
                acc[pl.ds(r0, MOE_CHUNK), :] += d

            @pl.when(j == n_ff - 1)
            def _():
                groups = jnp.minimum(n8 - r0, MOE_CHUNK) // SUBLANES

                def put(gi, c2):
                    out_copy(r0 + gi * SUBLANES).start()
                    return c2

                lax.fori_loop(0, groups, put, 0)

            return carry

        lax.fori_loop(0, (n + MOE_CHUNK - 1) // MOE_CHUNK, chunk, 0)

        @pl.when(j == n_ff - 1)
        def _():
            def wait_out(gi, c2):
                out_copy(gi * SUBLANES).wait()
                return c2

            lax.fori_loop(0, n8 // SUBLANES, wait_out, 0)

    @pl.when((q == n_items - 1) & (j == n_ff - 1))
    def _():
        used = ist_ref[n_items]
        n_tail = (y_hbm.shape[0] - used) // SUBLANES

        def tail_copy(gi):
            dst = pl.multiple_of(used + gi * SUBLANES, SUBLANES)
            return pltpu.make_async_copy(zrows, y_hbm.at[pl.ds(dst, SUBLANES), :], sems.at[1])

        def put(gi, c2):
            tail_copy(gi).start()
            return c2

        def wait(gi, c2):
            tail_copy(gi).wait()
            return c2

        lax.fori_loop(0, n_tail, put, 0)
        lax.fori_loop(0, n_tail, wait, 0)


def _moe_experts(x, plan, w_gu, b_gu, w_down, b_down, layer, n_items):
    item_e, item_start, item_n, row_tok, _, n_rows = plan
    d = x.shape[1]
    d_ff = w_down.shape[2]
    n_e = w_gu.shape[1]
    n_ff = d_ff // FF_TILE
    assert d_ff % FF_TILE == 0 and MOE_CAP % MOE_CHUNK == 0

    def ff_tile(q, j, inn):
        return jnp.where(inn[q] > 0, j, n_ff - 1)

    bg4 = b_gu.reshape(b_gu.shape[0], n_e, 1, 2 * d_ff)
    bd4 = b_down.reshape(b_down.shape[0], n_e, 1, d)
    grid_spec = pltpu.PrefetchScalarGridSpec(
        num_scalar_prefetch=4,
        grid=(n_items, n_ff),
        in_specs=[
            pl.BlockSpec(memory_space=pl.ANY),
            pl.BlockSpec((None, None, d, FF_TILE), lambda q, j, ie, ist, inn, tok: (layer, ie[q], 0, ff_tile(q, j, inn))),
            pl.BlockSpec((None, None, d, FF_TILE),
                         lambda q, j, ie, ist, inn, tok: (layer, ie[q], 0, n_ff + ff_tile(q, j, inn))),
            pl.BlockSpec((None, None, FF_TILE, d), lambda q, j, ie, ist, inn, tok: (layer, ie[q], ff_tile(q, j, inn), 0)),
            pl.BlockSpec((None, None, 1, FF_TILE), lambda q, j, ie, ist, inn, tok: (layer, ie[q], 0, ff_tile(q, j, inn))),
            pl.BlockSpec((None, None, 1, FF_TILE),
                         lambda q, j, ie, ist, inn, tok: (layer, ie[q], 0, n_ff + ff_tile(q, j, inn))),
            pl.BlockSpec((None, None, 1, d), lambda q, j, ie, ist, inn, tok: (layer, ie[q], 0, 0)),
        ],
        out_specs=pl.BlockSpec(memory_space=pl.ANY),
        scratch_shapes=[
            pltpu.VMEM((MOE_CAP, d), F32),
            pltpu.VMEM((MOE_CAP, d), BF16),
            pltpu.VMEM((MOE_CAP, d), F32),
            pltpu.VMEM((d, FF_TILE), BF16),
            pltpu.VMEM((d, FF_TILE), BF16),
            pltpu.VMEM((FF_TILE, d), BF16),
            pltpu.VMEM((SUBLANES, d), F32),
            pltpu.SemaphoreType.DMA((2,)),
        ],
    )
    return pl.pallas_call(
        functools.partial(_moe_body, n_items=n_items, n_ff=n_ff),
        grid_spec=grid_spec,
        out_shape=jax.ShapeDtypeStruct((n_rows, d), F32),
        compiler_params=_params(2),
        name="moe_experts",
    )(item_e, item_start, item_n, row_tok, x, w_gu, w_gu, w_down, bg4, bg4, bd4)


def _combine_body(pos_ref, y_hbm, gate_ref, x_ref, g_ref, b_ref, o_ref, ybuf, sem, *, alpha, n_tok):
    s = pl.program_id(0)
    tm = x_ref.shape[0]
    n_valid = jnp.clip(n_tok - s * tm, 0, tm)

    def copy(r, k, src_row):
        return pltpu.make_async_copy(y_hbm.at[pl.ds(src_row, 1), :], ybuf.at[k, pl.ds(r, 1), :], sem.at[0])

    @pl.when(s == 0)
    def _():
        ybuf[...] = jnp.zeros(ybuf.shape, F32)

    def issue(r, c):
        base = (s * tm + r) * TOP_K
        for k in range(TOP_K):
            copy(r, k, pos_ref[base + k]).start()
        return c

    lax.fori_loop(0, n_valid, issue, 0)

    def wait(r, c):
        for k in range(TOP_K):
            copy(r, k, 0).wait()
        return c

    lax.fori_loop(0, n_valid, wait, 0)
    gates = gate_ref[...]
    y = gates[:, 0:1] * ybuf[0]
    for k in range(1, TOP_K):
        y = y + gates[:, k:k + 1] * ybuf[k]
    o_ref[...] = _layer_norm(alpha * x_ref[...] + y, g_ref[...], b_ref[...])


def _combine_ln(yr, pos, gates_tk, x, g, b, *, alpha, n_tok):
    t_pad, d = x.shape
    tm = COMBINE_TILE
    grid_spec = pltpu.PrefetchScalarGridSpec(
        num_scalar_prefetch=1,
        grid=(t_pad // tm,),
        in_specs=[
            pl.BlockSpec(memory_space=pl.ANY),
            pl.BlockSpec((tm, TOP_K), lambda s, p: (s, 0)),
            pl.BlockSpec((tm, d), lambda s, p: (s, 0)),
            _resident((1, d), lambda s, p: (0, 0)),
            _resident((1, d), lambda s, p: (0, 0)),
        ],
        out_specs=pl.BlockSpec((tm, d), lambda s, p: (s, 0)),
        scratch_shapes=[pltpu.VMEM((TOP_K, tm, d), F32), pltpu.SemaphoreType.DMA((1,))],
    )
    return pl.pallas_call(
        functools.partial(_combine_body, alpha=alpha, n_tok=n_tok),
        grid_spec=grid_spec,
        out_shape=jax.ShapeDtypeStruct((t_pad, d), F32),
        compiler_params=_params(1),
        name="combine_ln",
    )(pos, yr, gates_tk, x, g.reshape(1, d), b.reshape(1, d))


def _top_blocks(gate, col, n_valid_cols, n_cols):
    sel = jnp.zeros(gate.shape, F32)
    g = gate
    for _ in range(MOBA_TOPK):
        m = jnp.max(g, axis=1, keepdims=True)
        idx = jnp.min(jnp.where(g == m, col, float(n_cols)), axis=1, keepdims=True)
        pick = col == idx
        sel = jnp.where(pick, jnp.where(col < n_valid_cols, 1.0, sel), sel)
        g = jnp.where(pick, -jnp.inf, g)
    return sel


def _moba_body(q_ref, k_ref, v_ref, o_ref, kb, vb, kmean, *, n_blocks):
    i = pl.program_id(2)
    blk = MOBA_BLOCK
    scale = HEAD_DIM ** -0.5

    @pl.when(i == 0)
    def _():
        kb[...] = k_ref[...].astype(BF16)
        vb[...] = v_ref[...].astype(BF16)
        for jb in range(n_blocks):
            kmean[jb:jb + 1, :] = jnp.mean(k_ref[jb * blk:(jb + 1) * blk, :], axis=0, keepdims=True)

    q = q_ref[...]
    col = lax.broadcasted_iota(jnp.int32, (blk, n_blocks), 1).astype(F32)
    gate = _dot_nt(q, kmean[...], precision=lax.Precision.HIGHEST)
    own_f = i.astype(F32)
    gate = jnp.where(col < own_f, gate, NEG_INF)
    sel = _top_blocks(gate, col, own_f, n_blocks)

    qb = q.astype(BF16)
    own = pl.multiple_of(i * blk, blk)
    sc = _dot_nt(qb, kb[pl.ds(own, blk), :]) * scale
    rq = lax.broadcasted_iota(jnp.int32, (blk, blk), 0)
    ck = lax.broadcasted_iota(jnp.int32, (blk, blk), 1)
    sc = jnp.where(ck <= rq, sc, NEG_INF)
    m0 = jnp.max(sc, axis=-1, keepdims=True)
    p = jnp.exp(sc - m0)
    l0 = jnp.sum(p, axis=-1, keepdims=True)
    a0 = jnp.dot(p.astype(BF16), vb[pl.ds(own, blk), :], preferred_element_type=F32)

    def past(jb, carry):
        m, l, a = carry
        r0 = pl.multiple_of(jb * blk, blk)
        s_j = _dot_nt(qb, kb[pl.ds(r0, blk), :]) * scale
        picked = jnp.sum(jnp.where(col == jb.astype(F32), sel, 0.0), axis=1, keepdims=True)
        s_j = jnp.where(picked > 0.0, s_j, NEG_INF)
        m_new = jnp.maximum(m, jnp.max(s_j, axis=-1, keepdims=True))
        corr = jnp.exp(m - m_new)
        p_j = jnp.exp(s_j - m_new)
        l = corr * l + jnp.sum(p_j, axis=-1, keepdims=True)
        a = corr * a + jnp.dot(p_j.astype(BF16), vb[pl.ds(r0, blk), :], preferred_element_type=F32)
        return m_new, l, a

    _, l_fin, a_fin = lax.fori_loop(0, i, past, (m0, l0, a0))
    o_ref[...] = (a_fin / l_fin).astype(BF16)


def _moba_prompt(hq, kf, vf, *, n_seq, seq_len, n_heads):
    blk = MOBA_BLOCK
    n_blocks = seq_len // blk
    assert seq_len % blk == 0 and n_blocks >= MOBA_TOPK
    return pl.pallas_call(
        functools.partial(_moba_body, n_blocks=n_blocks),
        grid=(n_seq, n_heads, n_blocks),
        in_specs=[
            pl.BlockSpec((blk, HEAD_DIM), lambda b, h, i: (b * n_blocks + i, h)),
            pl.BlockSpec((seq_len, HEAD_DIM), lambda b, h, i: (b, h)),
            pl.BlockSpec((seq_len, HEAD_DIM), lambda b, h, i: (b, h)),
        ],
        out_specs=pl.BlockSpec((blk, HEAD_DIM), lambda b, h, i: (b * n_blocks + i, h)),
        out_shape=jax.ShapeDtypeStruct((n_seq * seq_len, n_heads * HEAD_DIM), BF16),
        scratch_shapes=[
            pltpu.VMEM((seq_len, HEAD_DIM), BF16),
            pltpu.VMEM((seq_len, HEAD_DIM), BF16),
            pltpu.VMEM((n_blocks, HEAD_DIM), F32),
        ],
        compiler_params=_params(3),
        name="moba_prompt",
    )(hq, kf, vf)


def _page_means_body(pt_ref, *refs, pages_per_block):
    o_ref = refs[-1]
    tot = jnp.sum(refs[0][...], axis=0, keepdims=True)
    for r in refs[1:pages_per_block]:
        tot = tot + jnp.sum(r[...], axis=0, keepdims=True)
    o_ref[...] = tot * (1.0 / MOBA_BLOCK)


def _page_means(cache_k3, page_table_flat, *, db, n_pages):
    page, w = cache_k3.shape[1], cache_k3.shape[2]
    ppb = MOBA_BLOCK // page
    n_blocks = n_pages // ppb

    def page_spec(p):
        return pl.BlockSpec((None, page, w), lambda b, j, pt: (pt[b * n_pages + j * ppb + p], 0, 0))

    grid_spec = pltpu.PrefetchScalarGridSpec(
        num_scalar_prefetch=1,
        grid=(db, n_blocks),
        in_specs=[page_spec(p) for p in range(ppb)],
        out_specs=pl.BlockSpec((None, None, 1, w), lambda b, j, pt: (b, j, 0, 0)),
    )
    return pl.pallas_call(
        functools.partial(_page_means_body, pages_per_block=ppb),
        grid_spec=grid_spec,
        out_shape=jax.ShapeDtypeStruct((db, n_blocks, 1, w), F32),
        compiler_params=_params(2),
        name="page_means",
    )(page_table_flat, *([cache_k3] * ppb))


def _select_body(q_ref, km_ref, o_ref, *, n_heads):
    b = pl.program_id(0)
    n_blocks = km_ref.shape[1]
    qrow = q_ref[pl.ds(b, 1), :]
    col = lax.broadcasted_iota(jnp.int32, (SUBLANES, n_blocks), 1).astype(F32)
    lane = lax.broadcasted_iota(jnp.int32, (1, LANES), 1)
    for h in range(n_heads):
        hs = slice(h * HEAD_DIM, (h + 1) * HEAD_DIM)
        q8 = jnp.broadcast_to(qrow[:, hs], (SUBLANES, HEAD_DIM))
        g = _dot_nt(q8, km_ref[0][:, hs], precision=lax.Precision.HIGHEST)
        out = jnp.zeros((1, LANES), jnp.int32)
        for r in range(MOBA_TOPK):
            m = jnp.max(g, axis=1, keepdims=True)
            idx = jnp.min(jnp.where(g == m, col, float(n_blocks)), axis=1, keepdims=True)
            g = jnp.where(col == idx, -jnp.inf, g)
            out = jnp.where(lane == r, idx[0:1, :].astype(jnp.int32), out)
        o_ref[0, h:h + 1, :] = out


def _moba_select(hq, kmean_s, *, row0, db, n_heads):
    n_blocks, w = kmean_s.shape[1], kmean_s.shape[2]
    assert n_blocks >= MOBA_TOPK and row0 % db == 0
    return pl.pallas_call(
        functools.partial(_select_body, n_heads=n_heads),
        grid=(db,),
        in_specs=[
            pl.BlockSpec((db, w), lambda b: (row0 // db, 0)),
            pl.BlockSpec((1, n_blocks, w), lambda b: (b, 0, 0)),
        ],
        out_specs=pl.BlockSpec((1, n_heads, LANES), lambda b: (b, 0, 0)),
        out_shape=jax.ShapeDtypeStruct((db, n_heads, LANES), jnp.int32),
        compiler_params=_params(1),
        name="moba_select",
    )(hq, kmean_s)


def _decode_attn_body(pg_ref, q_ref, kn_ref, vn_ref, k_ref, v_ref, o_ref, m_s, l_s, a_s, *, n_steps):
    s = pl.program_id(2)
    scale = HEAD_DIM ** -0.5
    q = q_ref[...]

    @pl.when(s == 0)
    def _():
        s_own = jnp.sum(q * kn_ref[...], axis=-1, keepdims=True) * scale
        m_s[...] = jnp.broadcast_to(s_own, m_s.shape)
        l_s[...] = jnp.ones(l_s.shape, F32)
        a_s[...] = jnp.broadcast_to(vn_ref[...], a_s.shape)

    q8 = jnp.broadcast_to(q, (SUBLANES, HEAD_DIM)).astype(BF16)
    sc = _dot_nt(q8, k_ref[...].astype(BF16)) * scale
    m_old = m_s[:, 0:1]
    m_new = jnp.maximum(m_old, jnp.max(sc, axis=-1, keepdims=True))
    corr = jnp.exp(m_old - m_new)
    p = jnp.exp(sc - m_new)
    l_new = corr * l_s[:, 0:1] + jnp.sum(p, axis=-1, keepdims=True)
    a_new = corr * a_s[...] + jnp.dot(p.astype(BF16), v_ref[...].astype(BF16), preferred_element_type=F32)
    m_s[...] = jnp.broadcast_to(m_new, m_s.shape)
    l_s[...] = jnp.broadcast_to(l_new, l_s.shape)
    a_s[...] = a_new

    @pl.when(s == n_steps - 1)
    def _():
        o_ref[...] = (a_new / l_new)[0:1, :]


def _moba_decode(pages, q4, kn4, vn4, cache_k3, cache_v3, *, db, n_heads, n_steps):
    page = cache_k3.shape[1]
    one = pl.BlockSpec((None, None, 1, HEAD_DIM), lambda b, h, s, pg: (b, h, 0, 0))
    paged = pl.BlockSpec((None, page, HEAD_DIM), lambda b, h, s, pg: (pg[(b * n_heads + h) * n_steps + s], 0, h))
    grid_spec = pltpu.PrefetchScalarGridSpec(
        num_scalar_prefetch=1,
        grid=(db, n_heads, n_steps),
        in_specs=[one, one, one, paged, paged],
        out_specs=one,
        scratch_shapes=[pltpu.VMEM((SUBLANES, LANES), F32), pltpu.VMEM((SUBLANES, LANES), F32),
                        pltpu.VMEM((SUBLANES, HEAD_DIM), F32)],
    )
    return pl.pallas_call(
        functools.partial(_decode_attn_body, n_steps=n_steps),
        grid_spec=grid_spec,
        out_shape=jax.ShapeDtypeStruct((db, n_heads, 1, HEAD_DIM), F32),
        compiler_params=_params(3),
        name="moba_decode",
    )(pages, q4, kn4, vn4, cache_k3, cache_v3)


def kernel(x_prompt, x_sample, cache_conv, cache_k, cache_v, cache_mem_k, cache_mem_v, page_table, mem_prompt,
           w_in_a, w_dw, b_dw, conv_ln_g, conv_ln_b, w_out_a, w_in_b, w_out_b, w_kv_shared, w_mem_kv,
           ln1_g, ln1_b, ln2_g, ln2_b, w_router, b_router, w_gu, b_gu, w_down, b_down):
    n_seq, seq_len, d = x_prompt.shape
    db, dseq, _ = x_sample.shape
    depth = ln1_g.shape[0]
    n_a = w_in_a.shape[0]
    n_e = w_router.shape[2]
    ch = w_dw.shape[2]
    moba_w = w_in_b.shape[2] - MEM_W
    n_heads = moba_w // HEAD_DIM
    n_mem = mem_prompt.shape[1]
    n_pool, page = cache_k.shape[0], cache_k.shape[1]
    n_pages = page_table.shape[1]
    past_len = n_pages * page
    alpha = (2 * depth) ** 0.25
    tl = ROW_TILE
    assert dseq == 1 and db % SUBLANES == 0 and db <= tl and seq_len % tl == 0
    assert ch + MEM_W == d and moba_w == ch and MOBA_BLOCK % page == 0 and past_len % MOBA_BLOCK == 0
    n_prompt = n_seq * seq_len
    n_tok = n_prompt + db
    t_pad = n_prompt + tl
    ppb = MOBA_BLOCK // page
    n_items = n_e + (n_tok * TOP_K) // MOE_CAP

    x = jnp.concatenate([x_prompt.reshape(n_prompt, d), x_sample.reshape(db, d),
                         jnp.zeros((tl - db, d), F32)], axis=0)

    pos = jnp.concatenate([jnp.tile(jnp.arange(seq_len, dtype=jnp.int32), n_seq),
                           jnp.full((db,), past_len, jnp.int32), jnp.zeros((tl - db,), jnp.int32)])
    half = HEAD_DIM // 2
    inv = ROPE_THETA ** (-jnp.arange(half, dtype=F32) / half)
    ang = pos.astype(F32)[:, None] * inv[None, :]
    rope = (jnp.concatenate([jnp.cos(ang), jnp.cos(ang)], axis=1),
            jnp.concatenate([-jnp.sin(ang), jnp.sin(ang)], axis=1))

    mem2 = mem_prompt.reshape(n_seq * n_mem, d)
    mkv = jnp.stack([_dense(mem2, w_mem_kv[l].astype(BF16), name="mem_kv") for l in range(depth)])
    new_mem_k = mkv[..., :MEM_W].reshape(depth, n_seq, n_mem, MEM_W)
    new_mem_v = mkv[..., MEM_W:].reshape(depth, n_seq, n_mem, MEM_W)
    mem_k_s = cache_mem_k.reshape(depth, db, n_mem, MEM_W)
    mem_v_s = cache_mem_v.reshape(depth, db, n_mem, MEM_W)

    cache_k3 = cache_k.reshape(n_pool, page, moba_w)
    cache_v3 = cache_v.reshape(n_pool, page, moba_w)
    pt_flat = page_table.reshape(-1).astype(jnp.int32)

    conv_tails, conv_new = [], []
    kf = vf = kmean_s = None
    for l in range(depth):
        if l < n_a:
            h = _dense(x, w_in_a[l].astype(BF16), name="in_proj_a")
            c_act, tail, us = _conv_module(h, jnp.swapaxes(cache_conv[l], 0, 1), w_dw[l], b_dw[l],
                                           conv_ln_g[l], conv_ln_b[l], n_seq=n_seq, seq_len=seq_len, ch=ch)
            conv_tails.append(tail[:, CONV_HALO - (CONV_W - 1):])
            conv_new.append(jnp.concatenate([cache_conv[l][:, 1:], us[:, None, :]], axis=1))
            m_act = _mem_attention(h, 2 * ch // MEM_W, new_mem_k[l], new_mem_v[l], mem_k_s[l], mem_v_s[l],
                                   n_seq=n_seq, seq_len=seq_len)
            a1, a1s, w_out = c_act, None, w_out_a[l]
        else:
            jb = l - n_a
            if kf is None:
                kv = _dense(x, w_kv_shared.astype(BF16), rope=rope, n_rope_heads=n_heads, name="kv_proj")
                kf, vf = kv[:, :moba_w], kv[:, moba_w:]
                kmean_s = _page_means(cache_k3, pt_flat, db=db, n_pages=n_pages).reshape(db, n_pages // ppb, moba_w)
            hq = _dense(x, w_in_b[jb].astype(BF16), rope=rope, n_rope_heads=n_heads, name="in_proj_b")
            o_prompt = _moba_prompt(hq, kf, vf, n_seq=n_seq, seq_len=seq_len, n_heads=n_heads)
            sel = _moba_select(hq, kmean_s, row0=n_prompt, db=db, n_heads=n_heads)[:, :, :MOBA_TOPK]
            blk_pages = sel[..., None] * ppb + jnp.arange(ppb, dtype=jnp.int32)
            pages = jnp.take_along_axis(page_table.astype(jnp.int32)[:, None, :],
                                        blk_pages.reshape(db, n_heads, MOBA_TOPK * ppb), axis=2)
            four = lambda a: a[n_prompt:n_tok].reshape(db, n_heads, 1, HEAD_DIM)
            o_s = _moba_decode(pages.reshape(-1), four(hq[:, :moba_w]), four(kf), four(vf), cache_k3, cache_v3,
                               db=db, n_heads=n_heads, n_steps=MOBA_TOPK * ppb)
            a1s = jnp.concatenate([o_s.reshape(db, moba_w), jnp.zeros((tl - db, moba_w), F32)], axis=0).astype(BF16)
            m_act = _mem_attention(hq, moba_w // MEM_W, new_mem_k[l], new_mem_v[l], mem_k_s[l], mem_v_s[l],
                                   n_seq=n_seq, seq_len=seq_len)
            a1, w_out = o_prompt, w_out_b[jb]
        x = _outproj_ln(a1, a1s, m_act, w_out.astype(BF16), x, ln1_g[l], ln1_b[l], alpha=alpha)

        top_e, gates = _router(x, w_router[l].T, b_router[l])
        plan = _route_plan(top_e, n_tok, n_e, n_items, t_pad)
        yr = _moe_experts(x, plan, w_gu, b_gu, w_down, b_down, l, n_items)
        x = _combine_ln(yr, plan[4], gates.T, x, ln2_g[l], ln2_b[l], alpha=alpha, n_tok=n_tok)

    y_prompt = x[:n_prompt].reshape(n_seq, seq_len, d)
    y_sample = x[n_prompt:n_tok].reshape(db, 1, d)
    heads4 = lambda a, n, s: a.reshape(n, s, n_heads, HEAD_DIM)
    mem5 = lambda a: a.reshape(depth, n_seq, n_mem, N_MEM_HEADS, HEAD_DIM)
    return (y_prompt, y_sample, jnp.stack(conv_tails), jnp.stack(conv_new),
            heads4(kf[:n_prompt], n_seq, seq_len), heads4(vf[:n_prompt], n_seq, seq_len),
            heads4(kf[n_prompt:n_tok], db, 1), heads4(vf[n_prompt:n_tok], db, 1),
            mem5(new_mem_k), mem5(new_mem_v))
```

```python
import functools

import jax
import jax.numpy as jnp
from jax import lax
from jax.experimental import pallas as pl
from jax.experimental.pallas import tpu as pltpu

F32 = jnp.float32
BF16 = jnp.bfloat16

HEAD_DIM = 128
N_MEM_HEADS = 4
MEM_W = N_MEM_HEADS * HEAD_DIM
CONV_W = 31
MOBA_BLOCK = 256
MOBA_TOPK = 3
ROPE_THETA = 10000.0
TOP_K = 4
SWIGLU_LIMIT = 7.0
SWIGLU_ALPHA = 1.702
LN_EPS = 1e-5
NEG_INF = -1e30

ROW_TILE = MOBA_BLOCK
CONV_HALO = 32
SUBLANES = 8
LANES = 128
FF_TILE = 512
MOE_CHUNK = 256
MOE_CAP = 1280
COMBINE_TILE = 128
MOBA_GROUP = 4
VMEM_LIMIT = 48 * 1024 * 1024
MOE_VMEM_LIMIT = 60 * 1024 * 1024


def _params(n_axes, vmem=VMEM_LIMIT):
    return pltpu.CompilerParams(dimension_semantics=("arbitrary",) * n_axes, vmem_limit_bytes=vmem)


def _resident(shape, index_map):
    return pl.BlockSpec(shape, index_map, pipeline_mode=pl.Buffered(1))


def _layer_norm(z, g, b):
    mu = jnp.mean(z, axis=-1, keepdims=True)
    zc = z - mu
    var = jnp.mean(zc * zc, axis=-1, keepdims=True)
    return zc * lax.rsqrt(var + LN_EPS) * g + b


def _dot_nt(a, b, **kw):
    return lax.dot_general(a, b, (((1,), (1,)), ((), ())), preferred_element_type=F32, **kw)


def _dense_body(x_ref, w_ref, *rest, n_rope_heads, col_chunk):
    if n_rope_heads:
        cos_ref, sin_ref, o_ref = rest
    else:
        (o_ref,) = rest
    x = x_ref[...].astype(BF16)
    n = o_ref.shape[1]
    for c0 in range(0, n, col_chunk):
        cw = min(col_chunk, n - c0)
        y = jnp.dot(x, w_ref[:, c0:c0 + cw], preferred_element_type=F32)
        for h0 in range(0, cw, HEAD_DIM):
            blk = y[:, h0:h0 + HEAD_DIM]
            if (c0 + h0) // HEAD_DIM < n_rope_heads:
                blk = blk * cos_ref[...] + pltpu.roll(blk, HEAD_DIM // 2, 1) * sin_ref[...]
            o_ref[:, c0 + h0:c0 + h0 + HEAD_DIM] = blk


def _dense(x, w, rope=None, n_rope_heads=0, name="dense"):
    m, k = x.shape
    n = w.shape[1]
    tm = min(ROW_TILE, m)
    assert m % tm == 0 and n % HEAD_DIM == 0
    in_specs = [pl.BlockSpec((tm, k), lambda i: (i, 0)), _resident((k, n), lambda i: (0, 0))]
    args = [x, w]
    if n_rope_heads:
        in_specs += [pl.BlockSpec((tm, HEAD_DIM), lambda i: (i, 0))] * 2
        args += list(rope)
    return pl.pallas_call(
        functools.partial(_dense_body, n_rope_heads=n_rope_heads, col_chunk=512),
        grid=(m // tm,),
        in_specs=in_specs,
        out_specs=pl.BlockSpec((tm, n), lambda i: (i, 0)),
        out_shape=jax.ShapeDtypeStruct((m, n), F32),
        compiler_params=_params(1),
        name=name,
    )(*args)


def _conv_body(a_ref, g_ref, prev_ref, w_ref, b_ref, cg_ref, cb_ref, c_ref, tail_ref, us_ref, uext, ybuf,
               *, tiles_per_seq, n_prompt_tiles, db):
    s = pl.program_id(0)
    i = s % tiles_per_seq
    tl, ch = a_ref.shape
    u = a_ref[...] * jax.nn.sigmoid(g_ref[...])

    def ln_swish(y):
        z = _layer_norm(y, cg_ref[...], cb_ref[...])
        return z * jax.nn.sigmoid(z)

    @pl.when(s < n_prompt_tiles)
    def _():
        @pl.when(i == 0)
        def _():
            uext[0:CONV_HALO, :] = jnp.zeros((CONV_HALO, ch), F32)

        uext[CONV_HALO:CONV_HALO + tl, :] = u

        def lane_chunk(cc, carry):
            c0 = pl.multiple_of(cc * LANES, LANES)
            acc = jnp.zeros((tl, LANES), F32) + b_ref[:, pl.ds(c0, LANES)]
            for k in range(CONV_W):
                off = k + CONV_HALO - (CONV_W - 1)
                acc = acc + w_ref[k:k + 1, pl.ds(c0, LANES)] * uext[off:off + tl, pl.ds(c0, LANES)]
            ybuf[:, pl.ds(c0, LANES)] = acc
            return carry

        lax.fori_loop(0, ch // LANES, lane_chunk, 0)
        c_ref[...] = ln_swish(ybuf[...]).astype(BF16)

        @pl.when(i == tiles_per_seq - 1)
        def _():
            tail_ref[0] = uext[tl:tl + CONV_HALO, :]

        uext[0:CONV_HALO, :] = uext[tl:tl + CONV_HALO, :]

    @pl.when(s == n_prompt_tiles)
    def _():
        us = u[0:db, :]
        acc = b_ref[...] + w_ref[CONV_W - 1:CONV_W, :] * us
        for k in range(CONV_W - 1):
            acc = acc + w_ref[k:k + 1, :] * prev_ref[k]
        y = ln_swish(acc)
        c_ref[...] = jnp.concatenate([y, jnp.zeros((tl - db, ch), F32)], axis=0).astype(BF16)
        us_ref[...] = us


def _conv_module(h, prev_t, w_dw, b_dw, cg, cb, *, n_seq, seq_len, ch):
    t_pad = h.shape[0]
    tl = ROW_TILE
    tiles_per_seq = seq_len // tl
    n_prompt_tiles = n_seq * tiles_per_seq
    db = prev_t.shape[1]
    row = lambda v: v.reshape(1, ch)
    return pl.pallas_call(
        functools.partial(_conv_body, tiles_per_seq=tiles_per_seq, n_prompt_tiles=n_prompt_tiles, db=db),
        grid=(n_prompt_tiles + 1,),
        in_specs=[
            pl.BlockSpec((tl, ch), lambda s: (s, 0)),
            pl.BlockSpec((tl, ch), lambda s: (s, 1)),
            _resident((CONV_W - 1, db, ch), lambda s: (0, 0, 0)),
            _resident((CONV_W, ch), lambda s: (0, 0)),
            _resident((1, ch), lambda s: (0, 0)),
            _resident((1, ch), lambda s: (0, 0)),
            _resident((1, ch), lambda s: (0, 0)),
        ],
        out_specs=[
            pl.BlockSpec((tl, ch), lambda s: (s, 0)),
            pl.BlockSpec((1, CONV_HALO, ch), lambda s: (jnp.minimum(s // tiles_per_seq, n_seq - 1), 0, 0)),
            pl.BlockSpec((db, ch), lambda s: (0, 0)),
        ],
        out_shape=[
            jax.ShapeDtypeStruct((t_pad, ch), BF16),
            jax.ShapeDtypeStruct((n_seq, CONV_HALO, ch), F32),
            jax.ShapeDtypeStruct((db, ch), F32),
        ],
        scratch_shapes=[pltpu.VMEM((CONV_HALO + tl, ch), F32), pltpu.VMEM((tl, ch), F32)],
        compiler_params=_params(1),
        name="conv_module",
    )(h, h, prev_t, w_dw, row(b_dw), row(cg), row(cb))


def _attend_full(q, k, v):
    sc = _dot_nt(q, k) * (HEAD_DIM ** -0.5)
    p = jnp.exp(sc - jnp.max(sc, axis=-1, keepdims=True))
    o = jnp.dot(p.astype(BF16), v, preferred_element_type=F32)
    return o / jnp.sum(p, axis=-1, keepdims=True)


def _memattn_body(q_ref, mkp_ref, mvp_ref, mks_ref, mvs_ref, o_ref, *, n_prompt_tiles, db):
    s = pl.program_id(0)
    tl = q_ref.shape[0]
    heads = [slice(h * HEAD_DIM, (h + 1) * HEAD_DIM) for h in range(N_MEM_HEADS)]

    @pl.when(s < n_prompt_tiles)
    def _():
        q = q_ref[...].astype(BF16)
        k = mkp_ref[0].astype(BF16)
        v = mvp_ref[0].astype(BF16)
        for hs in heads:
            o_ref[:, hs] = _attend_full(q[:, hs], k[:, hs], v[:, hs]).astype(BF16)

    @pl.when(s == n_prompt_tiles)
    def _():
        q = q_ref[0:db, :].astype(BF16)
        row = lax.broadcasted_iota(jnp.int32, (db, HEAD_DIM), 0)
        outs = []
        for hs in heads:
            acc = jnp.zeros((db, HEAD_DIM), F32)
            for b in range(db):
                o = _attend_full(q[:, hs], mks_ref[b][:, hs].astype(BF16), mvs_ref[b][:, hs].astype(BF16))
                acc = jnp.where(row == b, o, acc)
            outs.append(acc)
        full = jnp.concatenate(outs, axis=1)
        o_ref[...] = jnp.concatenate([full, jnp.zeros((tl - db, MEM_W), F32)], axis=0).astype(BF16)


def _mem_attention(h, q_col_block, mkp, mvp, mks, mvs, *, n_seq, seq_len):
    t_pad = h.shape[0]
    tl = ROW_TILE
    tiles_per_seq = seq_len // tl
    n_prompt_tiles = n_seq * tiles_per_seq
    db, n_mem = mks.shape[0], mks.shape[1]
    seq_of = lambda s: (jnp.minimum(s // tiles_per_seq, n_seq - 1), 0, 0)
    return pl.pallas_call(
        functools.partial(_memattn_body, n_prompt_tiles=n_prompt_tiles, db=db),
        grid=(n_prompt_tiles + 1,),
        in_specs=[
            pl.BlockSpec((tl, MEM_W), lambda s: (s, q_col_block)),
            pl.BlockSpec((1, n_mem, MEM_W), seq_of),
            pl.BlockSpec((1, n_mem, MEM_W), seq_of),
            _resident((db, n_mem, MEM_W), lambda s: (0, 0, 0)),
            _resident((db, n_mem, MEM_W), lambda s: (0, 0, 0)),
        ],
        out_specs=pl.BlockSpec((tl, MEM_W), lambda s: (s, 0)),
        out_shape=jax.ShapeDtypeStruct((t_pad, MEM_W), BF16),
        compiler_params=_params(1),
        name="mem_attention",
    )(h, mkp, mvp, mks, mvs)


def _outproj_body(*refs, alpha, n_prompt_tiles, has_decode_tile):
    if has_decode_tile:
        a1_ref, a1s_ref, a2_ref, w_ref, x_ref, g_ref, b_ref, o_ref, op_ref = refs
    else:
        a1_ref, a2_ref, w_ref, x_ref, g_ref, b_ref, o_ref, op_ref = refs
    s = pl.program_id(0)
    k1 = a1_ref.shape[1]
    half = op_ref.shape[1]

    def finish(a1):
        y = jnp.dot(a1, w_ref[0:k1, :], preferred_element_type=F32)
        y = y + jnp.dot(a2_ref[...], w_ref[k1:, :], preferred_element_type=F32)
        xn = _layer_norm(alpha * x_ref[...] + y, g_ref[...], b_ref[...])
        o_ref[...] = xn
        bits = pltpu.bitcast(xn.astype(BF16).astype(F32), jnp.uint32)
        op_ref[...] = lax.shift_right_logical(bits[:, 0:half], jnp.uint32(16)) | (bits[:, half:] & jnp.uint32(0xFFFF0000))

    if has_decode_tile:
        @pl.when(s < n_prompt_tiles)
        def _():
            finish(a1_ref[...])

        @pl.when(s == n_prompt_tiles)
        def _():
            finish(a1s_ref[...])
    else:
        finish(a1_ref[...])


def _outproj_ln(a1, a1s, a2, w, x, g, b, *, alpha):
    t_pad, d = x.shape
    tl = ROW_TILE
    n_prompt_tiles = t_pad // tl - 1
    k1 = a1.shape[1]
    k2 = a2.shape[1]
    has_decode_tile = a1s is not None
    last_a1 = a1.shape[0] // tl - 1
    in_specs = [pl.BlockSpec((tl, k1), lambda s: (jnp.minimum(s, last_a1), 0))]
    args = [a1]
    if has_decode_tile:
        in_specs.append(_resident((tl, k1), lambda s: (0, 0)))
        args.append(a1s)
    in_specs += [
        pl.BlockSpec((tl, k2), lambda s: (s, 0)),
        _resident((k1 + k2, d), lambda s: (0, 0)),
        pl.BlockSpec((tl, d), lambda s: (s, 0)),
        _resident((1, d), lambda s: (0, 0)),
        _resident((1, d), lambda s: (0, 0)),
    ]
    args += [a2, w, x, g.reshape(1, d), b.reshape(1, d)]
    return pl.pallas_call(
        functools.partial(_outproj_body, alpha=alpha, n_prompt_tiles=n_prompt_tiles,
                          has_decode_tile=has_decode_tile),
        grid=(n_prompt_tiles + 1,),
        in_specs=in_specs,
        out_specs=[pl.BlockSpec((tl, d), lambda s: (s, 0)), pl.BlockSpec((tl, d // 2), lambda s: (s, 0))],
        out_shape=[jax.ShapeDtypeStruct((t_pad, d), F32), jax.ShapeDtypeStruct((t_pad, d // 2), jnp.uint32)],
        compiler_params=_params(1),
        name="outproj_ln",
    )(*args)


def _router_body(x_ref, wt_ref, b_ref, e_ref, g_ref):
    logits = _dot_nt(wt_ref[...], x_ref[...], precision=lax.Precision.HIGHEST) + b_ref[...]
    n_e = logits.shape[0]
    eid = lax.broadcasted_iota(jnp.int32, logits.shape, 0).astype(F32)
    vals, idxs = [], []
    for _ in range(TOP_K):
        m = jnp.max(logits, axis=0, keepdims=True)
        idx = jnp.min(jnp.where(logits == m, eid, float(n_e)), axis=0, keepdims=True)
        vals.append(m)
        idxs.append(idx)
        logits = jnp.where(eid == idx, -jnp.inf, logits)
    ex = [jnp.exp(v - vals[0]) for v in vals]
    den = ex[0] + ex[1] + ex[2] + ex[3]
    e_ref[...] = jnp.concatenate(idxs, axis=0).astype(jnp.int32)
    g_ref[...] = jnp.concatenate([e / den for e in ex], axis=0)


def _router(x, w_router_t, b_router):
    t_pad, d = x.shape
    n_e = w_router_t.shape[0]
    tl = ROW_TILE
    return pl.pallas_call(
        _router_body,
        grid=(t_pad // tl,),
        in_specs=[
            pl.BlockSpec((tl, d), lambda i: (i, 0)),
            _resident((n_e, d), lambda i: (0, 0)),
            _resident((n_e, 1), lambda i: (0, 0)),
        ],
        out_specs=[pl.BlockSpec((TOP_K, tl), lambda i: (0, i))] * 2,
        out_shape=[jax.ShapeDtypeStruct((TOP_K, t_pad), jnp.int32), jax.ShapeDtypeStruct((TOP_K, t_pad), F32)],
        compiler_params=_params(1),
        name="router",
    )(x, w_router_t, b_router.reshape(n_e, 1))


def _route_plan(top_e, n_tok, n_e, n_items, t_pad):
    tk = n_tok * TOP_K
    e_flat = top_e[:, :n_tok].T.reshape(tk)
    onehot = (e_flat[:, None] == jnp.arange(n_e, dtype=jnp.int32)[None, :]).astype(jnp.int32)
    csum = jnp.cumsum(onehot, axis=0)
    rank = jnp.take_along_axis(csum, e_flat[:, None], axis=1)[:, 0] - 1
    counts = csum[-1]
    padded = (counts + SUBLANES - 1) // SUBLANES * SUBLANES
    seg_start = jnp.cumsum(padded) - padded
    dest = seg_start[e_flat] + rank
    n_rows = (tk + n_e * (SUBLANES - 1) + SUBLANES - 1) // SUBLANES * SUBLANES
    row_tok = jnp.zeros((n_rows,), jnp.int32).at[dest].set(jnp.arange(tk, dtype=jnp.int32) // TOP_K)
    items_per_e = (counts + MOE_CAP - 1) // MOE_CAP
    item_end = jnp.cumsum(items_per_e)
    total = item_end[-1]
    q = jnp.arange(n_items, dtype=jnp.int32)
    valid = q < total
    e_q = jnp.clip(jnp.searchsorted(item_end, q, side="right"), 0, n_e - 1).astype(jnp.int32)
    e_q = jnp.where(valid, e_q, e_q[jnp.maximum(total - 1, 0)])
    within = q - (item_end - items_per_e)[e_q]
    item_start = jnp.where(valid, seg_start[e_q] + within * MOE_CAP, 0).astype(jnp.int32)
    item_start = jnp.concatenate([item_start, jnp.sum(padded, keepdims=True).astype(jnp.int32)])
    item_n = jnp.where(valid, jnp.clip(counts[e_q] - within * MOE_CAP, 0, MOE_CAP), 0).astype(jnp.int32)
    pos = jnp.zeros((t_pad * TOP_K,), jnp.int32).at[:tk].set(dest.astype(jnp.int32))
    return e_q, item_start, item_n, row_tok, pos, n_rows


def _moe_body(ie_ref, ist_ref, in_ref, tok_ref,
              x_hbm, wg_ref, wu_ref, wd_ref, bg_ref, bu_ref, bd_ref, y_hbm,
              xstage, xb, acc, wgb, wub, wdb, zrows, sems, *, n_items, n_ff):
    q = pl.program_id(0)
    j = pl.program_id(1)
    n = in_ref[q]
    start = ist_ref[q]
    n8 = (n + SUBLANES - 1) // SUBLANES * SUBLANES
    j_issue = min(1, n_ff - 1)
    half = xstage.shape[1]

    def gather_copy(r, tok):
        return pltpu.make_async_copy(x_hbm.at[pl.ds(tok, 1), :], xstage.at[pl.ds(r, 1), :], sems.at[0])

    def issue_gather(item):
        base = ist_ref[item]
        groups = (in_ref[item] + SUBLANES - 1) // SUBLANES

        def body(gi, c):
            r0 = gi * SUBLANES
            for u in range(SUBLANES):
                gather_copy(r0 + u, tok_ref[base + r0 + u]).start(priority=u % 2)
            return c

        lax.fori_loop(0, groups, body, 0)

    @pl.when((q == 0) & (j == 0))
    def _():
        xstage[...] = jnp.zeros(xstage.shape, jnp.uint32)
        zrows[...] = jnp.zeros(zrows.shape, F32)
        issue_gather(0)

    @pl.when((j == 0) & (n > 0))
    def _():
        def wait_body(r, c):
            gather_copy(r, 0).wait()
            return c

        lax.fori_loop(0, n8, wait_body, 0)

        def cvt(c, carry):
            r0 = pl.multiple_of(c * MOE_CHUNK, MOE_CHUNK)
            w = xstage[pl.ds(r0, MOE_CHUNK), :]
            lo = pltpu.bitcast(lax.shift_left(w, jnp.uint32(16)), F32)
            hi = pltpu.bitcast(w & jnp.uint32(0xFFFF0000), F32)
            xb[pl.ds(r0, MOE_CHUNK), 0:half] = lo.astype(BF16)
            xb[pl.ds(r0, MOE_CHUNK), half:] = hi.astype(BF16)
            return carry

        lax.fori_loop(0, (n + MOE_CHUNK - 1) // MOE_CHUNK, cvt, 0)

    @pl.when((j == j_issue) & (q + 1 < n_items))
    def _():
        issue_gather(jnp.minimum(q + 1, n_items - 1))

    def out_copy(r0):
        return pltpu.make_async_copy(acc.at[pl.ds(pl.multiple_of(r0, SUBLANES), SUBLANES), :],
                                     y_hbm.at[pl.ds(pl.multiple_of(start + r0, SUBLANES), SUBLANES), :],
                                     sems.at[1])

    @pl.when(n > 0)
    def _():
        wgb[...] = wg_ref[...].astype(BF16)
        wub[...] = wu_ref[...].astype(BF16)
        wdb[...] = wd_ref[...].astype(BF16)

        def chunk(r0, rows):
            xc = xb[pl.ds(r0, rows), :]
            gate = jnp.dot(xc, wgb[...], preferred_element_type=F32) + bg_ref[...]
            lin = jnp.dot(xc, wub[...], preferred_element_type=F32) + bu_ref[...]
            glu = jnp.minimum(gate, SWIGLU_LIMIT)
            lin = jnp.clip(lin, -SWIGLU_LIMIT, SWIGLU_LIMIT)
            hid = glu * jax.nn.sigmoid(SWIGLU_ALPHA * glu) * (lin + 1.0)
            d = jnp.dot(hid.astype(BF16), wdb[...], preferred_element_type=F32)

            @pl.when(j == 0)
            def _():
                acc[pl.ds(r0, rows), :] = d + bd_ref[...]

            @pl.when(j > 0)
            def _():
                acc[pl.ds(r0, rows), :] += d

            @pl.when(j == n_ff - 1)
            def _():
                groups = jnp.minimum(n8 - r0, rows) // SUBLANES

                def put(gi, c2):
                    out_copy(r0 + gi * SUBLANES).start()
                    return c2

                lax.fori_loop(0, groups, put, 0)

        def full_chunk(c, carry):
            chunk(pl.multiple_of(c * MOE_CHUNK, MOE_CHUNK), MOE_CHUNK)
            return carry

        n_full = n // MOE_CHUNK
        rem = n - n_full * MOE_CHUNK
        lax.fori_loop(0, n_full, full_chunk, 0)
        r_rem = pl.multiple_of(n_full * MOE_CHUNK, MOE_CHUNK)

        @pl.when(rem > MOE_CHUNK // 2)
        def _():
            chunk(r_rem, MOE_CHUNK)

        @pl.when((rem > 0) & (rem <= MOE_CHUNK // 2))
        def _():
            chunk(r_rem, MOE_CHUNK // 2)

        @pl.when(j == n_ff - 1)
        def _():
            def wait_out(gi, c2):
                out_copy(gi * SUBLANES).wait()
                return c2

            lax.fori_loop(0, n8 // SUBLANES, wait_out, 0)

    @pl.when((q == n_items - 1) & (j == n_ff - 1))
    def _():
        used = ist_ref[n_items]
        n_tail = (y_hbm.shape[0] - used) // SUBLANES

        def tail_copy(gi):
            dst = pl.multiple_of(used + gi * SUBLANES, SUBLANES)
            return pltpu.make_async_copy(zrows, y_hbm.at[pl.ds(dst, SUBLANES), :], sems.at[1])

        def put(gi, c2):
            tail_copy(gi).start()
            return c2

        def wait(gi, c2):
            tail_copy(gi).wait()
            return c2

        lax.fori_loop(0, n_tail, put, 0)
        lax.fori_loop(0, n_tail, wait, 0)


def _moe_experts(xp, plan, w_gu, b_gu, w_down, b_down, layer, n_items):
    item_e, item_start, item_n, row_tok, _, n_rows = plan
    d = 2 * xp.shape[1]
    d_ff = w_down.shape[2]
    n_e = w_gu.shape[1]
    n_ff = d_ff // FF_TILE
    assert d_ff % FF_TILE == 0 and MOE_CAP % MOE_CHUNK == 0

    def ff_tile(q, j, inn):
        return jnp.where(inn[q] > 0, j, n_ff - 1)

    bg4 = b_gu.reshape(b_gu.shape[0], n_e, 1, 2 * d_ff)
    bd4 = b_down.reshape(b_down.shape[0], n_e, 1, d)
    grid_spec = pltpu.PrefetchScalarGridSpec(
        num_scalar_prefetch=4,
        grid=(n_items, n_ff),
        in_specs=[
            pl.BlockSpec(memory_space=pl.ANY),
            pl.BlockSpec((None, None, d, FF_TILE), lambda q, j, ie, ist, inn, tok: (layer, ie[q], 0, ff_tile(q, j, inn))),
            pl.BlockSpec((None, None, d, FF_TILE),
                         lambda q, j, ie, ist, inn, tok: (layer, ie[q], 0, n_ff + ff_tile(q, j, inn))),
            pl.BlockSpec((None, None, FF_TILE, d), lambda q, j, ie, ist, inn, tok: (layer, ie[q], ff_tile(q, j, inn), 0)),
            pl.BlockSpec((None, None, 1, FF_TILE), lambda q, j, ie, ist, inn, tok: (layer, ie[q], 0, ff_tile(q, j, inn))),
            pl.BlockSpec((None, None, 1, FF_TILE),
                         lambda q, j, ie, ist, inn, tok: (layer, ie[q], 0, n_ff + ff_tile(q, j, inn))),
            pl.BlockSpec((None, None, 1, d), lambda q, j, ie, ist, inn, tok: (layer, ie[q], 0, 0)),
        ],
        out_specs=pl.BlockSpec(memory_space=pl.ANY),
        scratch_shapes=[
            pltpu.VMEM((MOE_CAP, d // 2), jnp.uint32),
            pltpu.VMEM((MOE_CAP, d), BF16),
            pltpu.VMEM((MOE_CAP, d), F32),
            pltpu.VMEM((d, FF_TILE), BF16),
            pltpu.VMEM((d, FF_TILE), BF16),
            pltpu.VMEM((FF_TILE, d), BF16),
            pltpu.VMEM((SUBLANES, d), F32),
            pltpu.SemaphoreType.DMA((2,)),
        ],
    )
    return pl.pallas_call(
        functools.partial(_moe_body, n_items=n_items, n_ff=n_ff),
        grid_spec=grid_spec,
        out_shape=jax.ShapeDtypeStruct((n_rows, d), F32),
        compiler_params=_params(2, MOE_VMEM_LIMIT),
        name="moe_experts",
    )(item_e, item_start, item_n, row_tok, xp, w_gu, w_gu, w_down, bg4, bg4, bd4)


def _combine_body(pos_ref, y_hbm, gate_ref, x_ref, g_ref, b_ref, o_ref, ybuf, sem, *, alpha, n_tok):
    s = pl.program_id(0)
    tm = x_ref.shape[0]
    n_valid = jnp.clip(n_tok - s * tm, 0, tm)

    def copy(r, k, src_row):
        return pltpu.make_async_copy(y_hbm.at[pl.ds(src_row, 1), :], ybuf.at[k, pl.ds(r, 1), :], sem.at[0])

    @pl.when(s == 0)
    def _():
        ybuf[...] = jnp.zeros(ybuf.shape, F32)

    def issue(r, c):
        base = (s * tm + r) * TOP_K
        for k in range(TOP_K):
            copy(r, k, pos_ref[base + k]).start(priority=k % 2)
        return c

    lax.fori_loop(0, n_valid, issue, 0)

    def wait(r, c):
        for k in range(TOP_K):
            copy(r, k, 0).wait()
        return c

    lax.fori_loop(0, n_valid, wait, 0)
    gates = gate_ref[...]
    y = gates[:, 0:1] * ybuf[0]
    for k in range(1, TOP_K):
        y = y + gates[:, k:k + 1] * ybuf[k]
    o_ref[...] = _layer_norm(alpha * x_ref[...] + y, g_ref[...], b_ref[...])


def _combine_ln(yr, pos, gates_tk, x, g, b, *, alpha, n_tok):
    t_pad, d = x.shape
    tm = COMBINE_TILE
    grid_spec = pltpu.PrefetchScalarGridSpec(
        num_scalar_prefetch=1,
        grid=(t_pad // tm,),
        in_specs=[
            pl.BlockSpec(memory_space=pl.ANY),
            pl.BlockSpec((tm, TOP_K), lambda s, p: (s, 0)),
            pl.BlockSpec((tm, d), lambda s, p: (s, 0)),
            _resident((1, d), lambda s, p: (0, 0)),
            _resident((1, d), lambda s, p: (0, 0)),
        ],
        out_specs=pl.BlockSpec((tm, d), lambda s, p: (s, 0)),
        scratch_shapes=[pltpu.VMEM((TOP_K, tm, d), F32), pltpu.SemaphoreType.DMA((1,))],
    )
    return pl.pallas_call(
        functools.partial(_combine_body, alpha=alpha, n_tok=n_tok),
        grid_spec=grid_spec,
        out_shape=jax.ShapeDtypeStruct((t_pad, d), F32),
        compiler_params=_params(1),
        name="combine_ln",
    )(pos, yr, gates_tk, x, g.reshape(1, d), b.reshape(1, d))


def _top_blocks_t(gate, row, n_valid_rows, n_rows):
    sel = jnp.zeros(gate.shape, F32)
    g = gate
    for _ in range(MOBA_TOPK):
        m = jnp.max(g, axis=0, keepdims=True)
        idx = jnp.min(jnp.where(g == m, row, float(n_rows)), axis=0, keepdims=True)
        pick = row == idx
        sel = jnp.where(pick, jnp.where(row < n_valid_rows, 1.0, sel), sel)
        g = jnp.where(pick, -jnp.inf, g)
    return sel


def _moba_body(q_ref, k_ref, v_ref, o_ref, kb, vt, kmean, sel_s, *, n_blocks):
    i = pl.program_id(2)
    blk = MOBA_BLOCK
    scale = HEAD_DIM ** -0.5

    @pl.when(i == 0)
    def _():
        kb[...] = k_ref[...].astype(BF16)
        for jb in range(n_blocks):
            rows = slice(jb * blk, (jb + 1) * blk)
            kmean[jb:jb + 1, :] = jnp.mean(k_ref[rows, :], axis=0, keepdims=True)
            vt[:, rows] = v_ref[rows, :].T.astype(BF16)

    qt = q_ref[...].T
    row = lax.broadcasted_iota(jnp.int32, (n_blocks, blk), 0).astype(F32)
    gate = jnp.dot(kmean[...], qt, preferred_element_type=F32, precision=lax.Precision.HIGHEST)
    own_f = i.astype(F32)
    gate = jnp.where(row < own_f, gate, NEG_INF)
    sel_s[...] = _top_blocks_t(gate, row, own_f, n_blocks)

    qtb = qt.astype(BF16)
    own = pl.multiple_of(i * blk, blk)
    st = jnp.dot(kb[pl.ds(own, blk), :], qtb, preferred_element_type=F32) * scale
    kk = lax.broadcasted_iota(jnp.int32, (blk, blk), 0)
    qq = lax.broadcasted_iota(jnp.int32, (blk, blk), 1)
    st = jnp.where(kk <= qq, st, NEG_INF)
    m0 = jnp.max(st, axis=0, keepdims=True)
    p = jnp.exp(st - m0)
    l0 = jnp.sum(p, axis=0, keepdims=True)
    a0 = jnp.dot(vt[:, pl.ds(own, blk)], p.astype(BF16), preferred_element_type=F32)

    def past(t, carry):
        m, l, a = carry
        starts, scores = [], []
        m_new = m
        for u in range(MOBA_GROUP):
            jb = jnp.minimum(t * MOBA_GROUP + u, n_blocks - 1)
            r0 = pl.multiple_of(jb * blk, blk)
            s_j = jnp.dot(kb[pl.ds(r0, blk), :], qtb, preferred_element_type=F32) * scale
            s_j = jnp.where(sel_s[pl.ds(jb, 1), :] > 0.0, s_j, NEG_INF)
            m_new = jnp.maximum(m_new, jnp.max(s_j, axis=0, keepdims=True))
            starts.append(r0)
            scores.append(s_j)
        corr = jnp.exp(m - m_new)
        l = corr * l
        a = corr * a
        for r0, s_j in zip(starts, scores):
            p_j = jnp.exp(s_j - m_new)
            l = l + jnp.sum(p_j, axis=0, keepdims=True)
            a = a + jnp.dot(vt[:, pl.ds(r0, blk)], p_j.astype(BF16), preferred_element_type=F32)
        return m_new, l, a

    n_trips = (i + MOBA_GROUP - 1) // MOBA_GROUP
    _, l_fin, a_fin = lax.fori_loop(0, n_trips, past, (m0, l0, a0))
    o_ref[...] = (a_fin / l_fin).T.astype(BF16)


def _moba_prompt(hq, kf, vf, *, n_seq, seq_len, n_heads):
    blk = MOBA_BLOCK
    n_blocks = seq_len // blk
    assert seq_len % blk == 0 and n_blocks >= MOBA_TOPK
    return pl.pallas_call(
        functools.partial(_moba_body, n_blocks=n_blocks),
        grid=(n_seq, n_heads, n_blocks),
        in_specs=[
            pl.BlockSpec((blk, HEAD_DIM), lambda b, h, i: (b * n_blocks + i, h)),
            pl.BlockSpec((seq_len, HEAD_DIM), lambda b, h, i: (b, h)),
            pl.BlockSpec((seq_len, HEAD_DIM), lambda b, h, i: (b, h)),
        ],
        out_specs=pl.BlockSpec((blk, HEAD_DIM), lambda b, h, i: (b * n_blocks + i, h)),
        out_shape=jax.ShapeDtypeStruct((n_seq * seq_len, n_heads * HEAD_DIM), BF16),
        scratch_shapes=[
            pltpu.VMEM((seq_len, HEAD_DIM), BF16),
            pltpu.VMEM((HEAD_DIM, seq_len), BF16),
            pltpu.VMEM((n_blocks, HEAD_DIM), F32),
            pltpu.VMEM((n_blocks, blk), F32),
        ],
        compiler_params=_params(3),
        name="moba_prompt",
    )(hq, kf, vf)


def _page_means_body(pt_ref, *refs, pages_per_block):
    o_ref = refs[-1]
    j = pl.program_id(1)
    tot = jnp.sum(refs[0][...], axis=0)
    for r in refs[1:pages_per_block]:
        tot = tot + jnp.sum(r[...], axis=0)
    o_ref[j] = tot * (1.0 / MOBA_BLOCK)


def _page_means(cache_k, page_table_flat, *, db, n_pages):
    _, page, n_heads, hd = cache_k.shape
    ppb = MOBA_BLOCK // page
    n_blocks = n_pages // ppb

    def page_spec(p):
        return pl.BlockSpec((None, page, n_heads, hd), lambda b, j, pt: (pt[b * n_pages + j * ppb + p], 0, 0, 0))

    grid_spec = pltpu.PrefetchScalarGridSpec(
        num_scalar_prefetch=1,
        grid=(db, n_blocks),
        in_specs=[page_spec(p) for p in range(ppb)],
        out_specs=pl.BlockSpec((None, n_blocks, n_heads, hd), lambda b, j, pt: (b, 0, 0, 0)),
    )
    return pl.pallas_call(
        functools.partial(_page_means_body, pages_per_block=ppb),
        grid_spec=grid_spec,
        out_shape=jax.ShapeDtypeStruct((db, n_blocks, n_heads, hd), F32),
        compiler_params=_params(2),
        name="page_means",
    )(page_table_flat, *([cache_k] * ppb))


def _select_body(q_ref, km_ref, o_ref, *, n_heads):
    b = pl.program_id(0)
    n_blocks = km_ref.shape[1]
    qrow = q_ref[pl.ds(b, 1), :]
    col = lax.broadcasted_iota(jnp.int32, (SUBLANES, n_blocks), 1).astype(F32)
    lane = lax.broadcasted_iota(jnp.int32, (1, LANES), 1)
    for h in range(n_heads):
        hs = slice(h * HEAD_DIM, (h + 1) * HEAD_DIM)
        q8 = jnp.broadcast_to(qrow[:, hs], (SUBLANES, HEAD_DIM))
        g = _dot_nt(q8, km_ref[h], precision=lax.Precision.HIGHEST)
        out = jnp.zeros((1, LANES), jnp.int32)
        for r in range(MOBA_TOPK):
            m = jnp.max(g, axis=1, keepdims=True)
            idx = jnp.min(jnp.where(g == m, col, float(n_blocks)), axis=1, keepdims=True)
            g = jnp.where(col == idx, -jnp.inf, g)
            out = jnp.where(lane == r, idx[0:1, :].astype(jnp.int32), out)
        o_ref[0, h:h + 1, :] = out


def _moba_select(hq, kmean_hs, *, row0, db, n_heads):
    n_blocks = kmean_hs.shape[2]
    w = n_heads * HEAD_DIM
    assert n_blocks >= MOBA_TOPK and row0 % db == 0
    return pl.pallas_call(
        functools.partial(_select_body, n_heads=n_heads),
        grid=(db,),
        in_specs=[
            pl.BlockSpec((db, w), lambda b: (row0 // db, 0)),
            pl.BlockSpec((None, n_heads, n_blocks, HEAD_DIM), lambda b: (b, 0, 0, 0)),
        ],
        out_specs=pl.BlockSpec((1, n_heads, LANES), lambda b: (b, 0, 0)),
        out_shape=jax.ShapeDtypeStruct((db, n_heads, LANES), jnp.int32),
        compiler_params=_params(1),
        name="moba_select",
    )(hq, kmean_hs)


def _decode_attn_body(pg_ref, q_ref, kn_ref, vn_ref, k_hbm, v_hbm, o_ref, kbuf, vbuf, sems,
                      *, n_heads, n_sel, n_steps):
    g = pl.program_id(0)
    page = kbuf.shape[2]
    scale = HEAD_DIM ** -0.5

    def copies(step, slot):
        h = step % n_heads
        out = []
        for s in range(n_sel):
            pg = pg_ref[step * n_sel + s]
            out.append(pltpu.make_async_copy(k_hbm.at[pg, :, h, :], kbuf.at[slot, s], sems.at[0, slot]))
            out.append(pltpu.make_async_copy(v_hbm.at[pg, :, h, :], vbuf.at[slot, s], sems.at[1, slot]))
        return out

    slot = g % 2

    @pl.when(g == 0)
    def _():
        for c in copies(0, 0):
            c.start()

    @pl.when(g + 1 < n_steps)
    def _():
        for c in copies(g + 1, 1 - slot):
            c.start()

    for c in copies(g, slot):
        c.wait()

    q = q_ref[...]
    q8 = jnp.broadcast_to(q, (SUBLANES, HEAD_DIM)).astype(BF16)
    k = kbuf[slot].reshape(n_sel * page, HEAD_DIM).astype(BF16)
    v = vbuf[slot].reshape(n_sel * page, HEAD_DIM).astype(BF16)
    sc = _dot_nt(q8, k) * scale
    s_own = jnp.sum(q * kn_ref[...], axis=-1, keepdims=True) * scale
    m = jnp.maximum(jnp.max(sc, axis=-1, keepdims=True), s_own)
    p = jnp.exp(sc - m)
    p_own = jnp.exp(s_own - m)
    den = jnp.sum(p, axis=-1, keepdims=True) + p_own
    acc = jnp.dot(p.astype(BF16), v, preferred_element_type=F32) + p_own * vn_ref[...]
    o_ref[...] = (acc / den)[0:1, :]


def _moba_decode(pages, q4, kn4, vn4, cache_k, cache_v, *, db, n_heads, n_sel):
    page = cache_k.shape[1]
    n_steps = db * n_heads
    one = pl.BlockSpec((None, None, 1, HEAD_DIM), lambda g, pg: (g // n_heads, g % n_heads, 0, 0))
    grid_spec = pltpu.PrefetchScalarGridSpec(
        num_scalar_prefetch=1,
        grid=(n_steps,),
        in_specs=[one, one, one, pl.BlockSpec(memory_space=pl.ANY), pl.BlockSpec(memory_space=pl.ANY)],
        out_specs=one,
        scratch_shapes=[pltpu.VMEM((2, n_sel, page, HEAD_DIM), F32), pltpu.VMEM((2, n_sel, page, HEAD_DIM), F32),
                        pltpu.SemaphoreType.DMA((2, 2))],
    )
    return pl.pallas_call(
        functools.partial(_decode_attn_body, n_heads=n_heads, n_sel=n_sel, n_steps=n_steps),
        grid_spec=grid_spec,
        out_shape=jax.ShapeDtypeStruct((db, n_heads, 1, HEAD_DIM), F32),
        compiler_params=_params(1),
        name="moba_decode",
    )(pages, q4, kn4, vn4, cache_k, cache_v)


def kernel(x_prompt, x_sample, cache_conv, cache_k, cache_v, cache_mem_k, cache_mem_v, page_table, mem_prompt,
           w_in_a, w_dw, b_dw, conv_ln_g, conv_ln_b, w_out_a, w_in_b, w_out_b, w_kv_shared, w_mem_kv,
           ln1_g, ln1_b, ln2_g, ln2_b, w_router, b_router, w_gu, b_gu, w_down, b_down):
    n_seq, seq_len, d = x_prompt.shape
    db, dseq, _ = x_sample.shape
    depth = ln1_g.shape[0]
    n_a = w_in_a.shape[0]
    n_e = w_router.shape[2]
    ch = w_dw.shape[2]
    moba_w = w_in_b.shape[2] - MEM_W
    n_heads = moba_w // HEAD_DIM
    n_mem = mem_prompt.shape[1]
    n_pool, page = cache_k.shape[0], cache_k.shape[1]
    n_pages = page_table.shape[1]
    past_len = n_pages * page
    alpha = (2 * depth) ** 0.25
    tl = ROW_TILE
    assert dseq == 1 and db % SUBLANES == 0 and db <= tl and seq_len % tl == 0
    assert ch + MEM_W == d and moba_w == ch and MOBA_BLOCK % page == 0 and past_len % MOBA_BLOCK == 0
    n_prompt = n_seq * seq_len
    n_tok = n_prompt + db
    t_pad = n_prompt + tl
    ppb = MOBA_BLOCK // page
    n_items = n_e + (n_tok * TOP_K) // MOE_CAP

    x = jnp.concatenate([x_prompt.reshape(n_prompt, d), x_sample.reshape(db, d),
                         jnp.zeros((tl - db, d), F32)], axis=0)

    pos = jnp.concatenate([jnp.tile(jnp.arange(seq_len, dtype=jnp.int32), n_seq),
                           jnp.full((db,), past_len, jnp.int32), jnp.zeros((tl - db,), jnp.int32)])
    half = HEAD_DIM // 2
    inv = ROPE_THETA ** (-jnp.arange(half, dtype=F32) / half)
    ang = pos.astype(F32)[:, None] * inv[None, :]
    rope = (jnp.concatenate([jnp.cos(ang), jnp.cos(ang)], axis=1),
            jnp.concatenate([-jnp.sin(ang), jnp.sin(ang)], axis=1))

    mem2 = mem_prompt.reshape(n_seq * n_mem, d)
    mkv = jnp.stack([_dense(mem2, w_mem_kv[l].astype(BF16), name="mem_kv") for l in range(depth)])
    new_mem_k = mkv[..., :MEM_W].reshape(depth, n_seq, n_mem, MEM_W)
    new_mem_v = mkv[..., MEM_W:].reshape(depth, n_seq, n_mem, MEM_W)
    mem_k_s = cache_mem_k.reshape(depth, db, n_mem, MEM_W)
    mem_v_s = cache_mem_v.reshape(depth, db, n_mem, MEM_W)

    pt_flat = page_table.reshape(-1).astype(jnp.int32)

    conv_tails, conv_new = [], []
    kf = vf = kmean_s = None
    for l in range(depth):
        if l < n_a:
            h = _dense(x, w_in_a[l].astype(BF16), name="in_proj_a")
            c_act, tail, us = _conv_module(h, jnp.swapaxes(cache_conv[l], 0, 1), w_dw[l], b_dw[l],
                                           conv_ln_g[l], conv_ln_b[l], n_seq=n_seq, seq_len=seq_len, ch=ch)
            conv_tails.append(tail[:, CONV_HALO - (CONV_W - 1):])
            conv_new.append(jnp.concatenate([cache_conv[l][:, 1:], us[:, None, :]], axis=1))
            m_act = _mem_attention(h, 2 * ch // MEM_W, new_mem_k[l], new_mem_v[l], mem_k_s[l], mem_v_s[l],
                                   n_seq=n_seq, seq_len=seq_len)
            a1, a1s, w_out = c_act, None, w_out_a[l]
        else:
            jb = l - n_a
            if kf is None:
                kv = _dense(x, w_kv_shared.astype(BF16), rope=rope, n_rope_heads=n_heads, name="kv_proj")
                kf, vf = kv[:, :moba_w], kv[:, moba_w:]
                kmean_s = jnp.swapaxes(_page_means(cache_k, pt_flat, db=db, n_pages=n_pages), 1, 2)
            hq = _dense(x, w_in_b[jb].astype(BF16), rope=rope, n_rope_heads=n_heads, name="in_proj_b")
            o_prompt = _moba_prompt(hq, kf, vf, n_seq=n_seq, seq_len=seq_len, n_heads=n_heads)
            sel = _moba_select(hq, kmean_s, row0=n_prompt, db=db, n_heads=n_heads)[:, :, :MOBA_TOPK]
            blk_pages = sel[..., None] * ppb + jnp.arange(ppb, dtype=jnp.int32)
            pages = jnp.take_along_axis(page_table.astype(jnp.int32)[:, None, :],
                                        blk_pages.reshape(db, n_heads, MOBA_TOPK * ppb), axis=2)
            four = lambda a: a[n_prompt:n_tok].reshape(db, n_heads, 1, HEAD_DIM)
            o_s = _moba_decode(pages.reshape(-1), four(hq[:, :moba_w]), four(kf), four(vf), cache_k, cache_v,
                               db=db, n_heads=n_heads, n_sel=MOBA_TOPK * ppb)
            a1s = jnp.concatenate([o_s.reshape(db, moba_w), jnp.zeros((tl - db, moba_w), F32)], axis=0).astype(BF16)
            m_act = _mem_attention(hq, moba_w // MEM_W, new_mem_k[l], new_mem_v[l], mem_k_s[l], mem_v_s[l],
                                   n_seq=n_seq, seq_len=seq_len)
            a1, w_out = o_prompt, w_out_b[jb]
        x, xp = _outproj_ln(a1, a1s, m_act, w_out.astype(BF16), x, ln1_g[l], ln1_b[l], alpha=alpha)

        top_e, gates = _router(x, w_router[l].T, b_router[l])
        plan = _route_plan(top_e, n_tok, n_e, n_items, t_pad)
        yr = _moe_experts(xp, plan, w_gu, b_gu, w_down, b_down, l, n_items)
        x = _combine_ln(yr, plan[4], gates.T, x, ln2_g[l], ln2_b[l], alpha=alpha, n_tok=n_tok)

    y_prompt = x[:n_prompt].reshape(n_seq, seq_len, d)
    y_sample = x[n_prompt:n_tok].reshape(db, 1, d)
    heads4 = lambda a, n, s: a.reshape(n, s, n_heads, HEAD_DIM)
    mem5 = lambda a: a.reshape(depth, n_seq, n_mem, N_MEM_HEADS, HEAD_DIM)
    return (y_prompt, y_sample, jnp.stack(conv_tails), jnp.stack(conv_new),
            heads4(kf[:n_prompt], n_seq, seq_len), heads4(vf[:n_prompt], n_seq, seq_len),
            heads4(kf[n_prompt:n_tok], db, 1), heads4(vf[n_prompt:n_tok], db, 1),
            mem5(new_mem_k), mem5(new_mem_v))
```

```python
import functools

import jax
import jax.numpy as jnp
from jax import lax
from jax.experimental import pallas as pl
from jax.experimental.pallas import tpu as pltpu

F32 = jnp.float32
BF16 = jnp.bfloat16

HEAD_DIM = 128
N_MEM_HEADS = 4
MEM_W = N_MEM_HEADS * HEAD_DIM
CONV_W = 31
MOBA_BLOCK = 256
MOBA_TOPK = 3
ROPE_THETA = 10000.0
TOP_K = 4
SWIGLU_LIMIT = 7.0
SWIGLU_ALPHA = 1.702
LN_EPS = 1e-5
NEG_INF = -1e30

ROW_TILE = MOBA_BLOCK
CONV_HALO = 32
SUBLANES = 8
LANES = 128
FF_TILE = 512
MOE_CHUNK = 256
MOE_CAP = 1280
COMBINE_TILE = 128
MOBA_GROUP = 4
VMEM_LIMIT = 48 * 1024 * 1024
MOE_VMEM_LIMIT = 60 * 1024 * 1024


def _params(n_axes, vmem=VMEM_LIMIT):
    return pltpu.CompilerParams(dimension_semantics=("arbitrary",) * n_axes, vmem_limit_bytes=vmem)


def _resident(shape, index_map):
    return pl.BlockSpec(shape, index_map, pipeline_mode=pl.Buffered(1))


def _layer_norm(z, g, b):
    mu = jnp.mean(z, axis=-1, keepdims=True)
    zc = z - mu
    var = jnp.mean(zc * zc, axis=-1, keepdims=True)
    return zc * lax.rsqrt(var + LN_EPS) * g + b


def _dot_nt(a, b, **kw):
    return lax.dot_general(a, b, (((1,), (1,)), ((), ())), preferred_element_type=F32, **kw)


def _dense_body(x_ref, w_ref, *rest, n_rope_heads, col_chunk, n_out):
    o_refs = rest[-n_out:]
    if n_rope_heads:
        cos_ref, sin_ref = rest[0], rest[1]
    x = x_ref[...].astype(BF16)
    n = w_ref.shape[1]
    out_w = o_refs[0].shape[1]
    for c0 in range(0, n, col_chunk):
        cw = min(col_chunk, n - c0)
        y = jnp.dot(x, w_ref[:, c0:c0 + cw], preferred_element_type=F32)
        for h0 in range(0, cw, HEAD_DIM):
            blk = y[:, h0:h0 + HEAD_DIM]
            col = c0 + h0
            if col // HEAD_DIM < n_rope_heads:
                blk = blk * cos_ref[...] + pltpu.roll(blk, HEAD_DIM // 2, 1) * sin_ref[...]
            o_refs[col // out_w][:, col % out_w:col % out_w + HEAD_DIM] = blk


def _dense(x, w, rope=None, n_rope_heads=0, n_out=1, name="dense"):
    m, k = x.shape
    n = w.shape[1]
    tm = min(ROW_TILE, m)
    assert m % tm == 0 and n % (n_out * HEAD_DIM) == 0
    in_specs = [pl.BlockSpec((tm, k), lambda i: (i, 0)), _resident((k, n), lambda i: (0, 0))]
    args = [x, w]
    if n_rope_heads:
        in_specs += [pl.BlockSpec((tm, HEAD_DIM), lambda i: (i, 0))] * 2
        args += list(rope)
    outs = pl.pallas_call(
        functools.partial(_dense_body, n_rope_heads=n_rope_heads, col_chunk=512, n_out=n_out),
        grid=(m // tm,),
        in_specs=in_specs,
        out_specs=[pl.BlockSpec((tm, n // n_out), lambda i: (i, 0))] * n_out,
        out_shape=[jax.ShapeDtypeStruct((m, n // n_out), F32)] * n_out,
        compiler_params=_params(1),
        name=name,
    )(*args)
    return outs[0] if n_out == 1 else outs


def _conv_body(a_ref, g_ref, prev_ref, w_ref, b_ref, cg_ref, cb_ref, c_ref, tail_ref, us_ref, uext, ybuf,
               *, tiles_per_seq, n_prompt_tiles, db):
    s = pl.program_id(0)
    i = s % tiles_per_seq
    tl, ch = a_ref.shape
    u = a_ref[...] * jax.nn.sigmoid(g_ref[...])

    def ln_swish(y):
        z = _layer_norm(y, cg_ref[...], cb_ref[...])
        return z * jax.nn.sigmoid(z)

    @pl.when(s < n_prompt_tiles)
    def _():
        @pl.when(i == 0)
        def _():
            uext[0:CONV_HALO, :] = jnp.zeros((CONV_HALO, ch), F32)

        uext[CONV_HALO:CONV_HALO + tl, :] = u

        def lane_chunk(cc, carry):
            c0 = pl.multiple_of(cc * LANES, LANES)
            acc = jnp.zeros((tl, LANES), F32) + b_ref[:, pl.ds(c0, LANES)]
            shifted = [uext[p:p + tl + CONV_HALO - (SUBLANES if p else 0), pl.ds(c0, LANES)]
                       for p in range(SUBLANES)]
            for k in range(CONV_W):
                off = k + CONV_HALO - (CONV_W - 1)
                a, p = divmod(off, SUBLANES)
                acc = acc + w_ref[k:k + 1, pl.ds(c0, LANES)] * shifted[p][a * SUBLANES:a * SUBLANES + tl]
            ybuf[:, pl.ds(c0, LANES)] = acc
            return carry

        lax.fori_loop(0, ch // LANES, lane_chunk, 0)
        c_ref[...] = ln_swish(ybuf[...]).astype(BF16)

        @pl.when(i == tiles_per_seq - 1)
        def _():
            tail_ref[0] = uext[tl:tl + CONV_HALO, :]

        uext[0:CONV_HALO, :] = uext[tl:tl + CONV_HALO, :]

    @pl.when(s == n_prompt_tiles)
    def _():
        us = u[0:db, :]
        acc = b_ref[...] + w_ref[CONV_W - 1:CONV_W, :] * us
        for k in range(CONV_W - 1):
            acc = acc + w_ref[k:k + 1, :] * prev_ref[k]
        y = ln_swish(acc)
        c_ref[...] = jnp.concatenate([y, jnp.zeros((tl - db, ch), F32)], axis=0).astype(BF16)
        us_ref[...] = us


def _conv_module(h, prev_t, w_dw, b_dw, cg, cb, *, n_seq, seq_len, ch):
    t_pad = h.shape[0]
    tl = ROW_TILE
    tiles_per_seq = seq_len // tl
    n_prompt_tiles = n_seq * tiles_per_seq
    db = prev_t.shape[1]
    row = lambda v: v.reshape(1, ch)
    return pl.pallas_call(
        functools.partial(_conv_body, tiles_per_seq=tiles_per_seq, n_prompt_tiles=n_prompt_tiles, db=db),
        grid=(n_prompt_tiles + 1,),
        in_specs=[
            pl.BlockSpec((tl, ch), lambda s: (s, 0)),
            pl.BlockSpec((tl, ch), lambda s: (s, 1)),
            _resident((CONV_W - 1, db, ch), lambda s: (0, 0, 0)),
            _resident((CONV_W, ch), lambda s: (0, 0)),
            _resident((1, ch), lambda s: (0, 0)),
            _resident((1, ch), lambda s: (0, 0)),
            _resident((1, ch), lambda s: (0, 0)),
        ],
        out_specs=[
            pl.BlockSpec((tl, ch), lambda s: (s, 0)),
            pl.BlockSpec((1, CONV_HALO, ch), lambda s: (jnp.minimum(s // tiles_per_seq, n_seq - 1), 0, 0)),
            pl.BlockSpec((db, ch), lambda s: (0, 0)),
        ],
        out_shape=[
            jax.ShapeDtypeStruct((t_pad, ch), BF16),
            jax.ShapeDtypeStruct((n_seq, CONV_HALO, ch), F32),
            jax.ShapeDtypeStruct((db, ch), F32),
        ],
        scratch_shapes=[pltpu.VMEM((CONV_HALO + tl, ch), F32), pltpu.VMEM((tl, ch), F32)],
        compiler_params=_params(1),
        name="conv_module",
    )(h, h, prev_t, w_dw, row(b_dw), row(cg), row(cb))


def _attend_full(q, k, v):
    sc = _dot_nt(q, k) * (HEAD_DIM ** -0.5)
    p = jnp.exp(sc - jnp.max(sc, axis=-1, keepdims=True))
    o = jnp.dot(p.astype(BF16), v, preferred_element_type=F32)
    return o / jnp.sum(p, axis=-1, keepdims=True)


def _memattn_body(q_ref, mkp_ref, mvp_ref, mks_ref, mvs_ref, o_ref, *, n_prompt_tiles, db):
    s = pl.program_id(0)
    tl = q_ref.shape[0]
    heads = [slice(h * HEAD_DIM, (h + 1) * HEAD_DIM) for h in range(N_MEM_HEADS)]

    @pl.when(s < n_prompt_tiles)
    def _():
        q = q_ref[...].astype(BF16)
        k = mkp_ref[0].astype(BF16)
        v = mvp_ref[0].astype(BF16)
        for hs in heads:
            o_ref[:, hs] = _attend_full(q[:, hs], k[:, hs], v[:, hs]).astype(BF16)

    @pl.when(s == n_prompt_tiles)
    def _():
        q = q_ref[0:db, :].astype(BF16)
        row = lax.broadcasted_iota(jnp.int32, (db, HEAD_DIM), 0)
        outs = []
        for hs in heads:
            acc = jnp.zeros((db, HEAD_DIM), F32)
            for b in range(db):
                o = _attend_full(q[:, hs], mks_ref[b][:, hs].astype(BF16), mvs_ref[b][:, hs].astype(BF16))
                acc = jnp.where(row == b, o, acc)
            outs.append(acc)
        full = jnp.concatenate(outs, axis=1)
        o_ref[...] = jnp.concatenate([full, jnp.zeros((tl - db, MEM_W), F32)], axis=0).astype(BF16)


def _mem_attention(h, q_col_block, mkp, mvp, mks, mvs, *, n_seq, seq_len):
    t_pad = h.shape[0]
    tl = ROW_TILE
    tiles_per_seq = seq_len // tl
    n_prompt_tiles = n_seq * tiles_per_seq
    db, n_mem = mks.shape[0], mks.shape[1]
    seq_of = lambda s: (jnp.minimum(s // tiles_per_seq, n_seq - 1), 0, 0)
    return pl.pallas_call(
        functools.partial(_memattn_body, n_prompt_tiles=n_prompt_tiles, db=db),
        grid=(n_prompt_tiles + 1,),
        in_specs=[
            pl.BlockSpec((tl, MEM_W), lambda s: (s, q_col_block)),
            pl.BlockSpec((1, n_mem, MEM_W), seq_of),
            pl.BlockSpec((1, n_mem, MEM_W), seq_of),
            _resident((db, n_mem, MEM_W), lambda s: (0, 0, 0)),
            _resident((db, n_mem, MEM_W), lambda s: (0, 0, 0)),
        ],
        out_specs=pl.BlockSpec((tl, MEM_W), lambda s: (s, 0)),
        out_shape=jax.ShapeDtypeStruct((t_pad, MEM_W), BF16),
        compiler_params=_params(1),
        name="mem_attention",
    )(h, mkp, mvp, mks, mvs)


def _outproj_body(*refs, alpha, n_prompt_tiles, has_decode_tile):
    if has_decode_tile:
        a1_ref, a1s_ref, a2_ref, w_ref, x_ref, g_ref, b_ref, o_ref, op_ref = refs
    else:
        a1_ref, a2_ref, w_ref, x_ref, g_ref, b_ref, o_ref, op_ref = refs
    s = pl.program_id(0)
    k1 = a1_ref.shape[1]
    half = op_ref.shape[1]

    def finish(a1):
        y = jnp.dot(a1, w_ref[0:k1, :], preferred_element_type=F32)
        y = y + jnp.dot(a2_ref[...], w_ref[k1:, :], preferred_element_type=F32)
        xn = _layer_norm(alpha * x_ref[...] + y, g_ref[...], b_ref[...])
        o_ref[...] = xn
        bits = pltpu.bitcast(xn.astype(BF16).astype(F32), jnp.uint32)
        op_ref[...] = lax.shift_right_logical(bits[:, 0:half], jnp.uint32(16)) | (bits[:, half:] & jnp.uint32(0xFFFF0000))

    if has_decode_tile:
        @pl.when(s < n_prompt_tiles)
        def _():
            finish(a1_ref[...])

        @pl.when(s == n_prompt_tiles)
        def _():
            finish(a1s_ref[...])
    else:
        finish(a1_ref[...])


def _outproj_ln(a1, a1s, a2, w, x, g, b, *, alpha):
    t_pad, d = x.shape
    tl = ROW_TILE
    n_prompt_tiles = t_pad // tl - 1
    k1 = a1.shape[1]
    k2 = a2.shape[1]
    has_decode_tile = a1s is not None
    last_a1 = a1.shape[0] // tl - 1
    in_specs = [pl.BlockSpec((tl, k1), lambda s: (jnp.minimum(s, last_a1), 0))]
    args = [a1]
    if has_decode_tile:
        in_specs.append(_resident((tl, k1), lambda s: (0, 0)))
        args.append(a1s)
    in_specs += [
        pl.BlockSpec((tl, k2), lambda s: (s, 0)),
        _resident((k1 + k2, d), lambda s: (0, 0)),
        pl.BlockSpec((tl, d), lambda s: (s, 0)),
        _resident((1, d), lambda s: (0, 0)),
        _resident((1, d), lambda s: (0, 0)),
    ]
    args += [a2, w, x, g.reshape(1, d), b.reshape(1, d)]
    return pl.pallas_call(
        functools.partial(_outproj_body, alpha=alpha, n_prompt_tiles=n_prompt_tiles,
                          has_decode_tile=has_decode_tile),
        grid=(n_prompt_tiles + 1,),
        in_specs=in_specs,
        out_specs=[pl.BlockSpec((tl, d), lambda s: (s, 0)), pl.BlockSpec((tl, d // 2), lambda s: (s, 0))],
        out_shape=[jax.ShapeDtypeStruct((t_pad, d), F32), jax.ShapeDtypeStruct((t_pad, d // 2), jnp.uint32)],
        compiler_params=_params(1),
        name="outproj_ln",
    )(*args)


def _router_body(x_ref, wt_ref, b_ref, e_ref, g_ref):
    logits = _dot_nt(wt_ref[...], x_ref[...], precision=lax.Precision.HIGHEST) + b_ref[...]
    n_e = logits.shape[0]
    eid = lax.broadcasted_iota(jnp.int32, logits.shape, 0).astype(F32)
    vals, idxs = [], []
    for _ in range(TOP_K):
        m = jnp.max(logits, axis=0, keepdims=True)
        idx = jnp.min(jnp.where(logits == m, eid, float(n_e)), axis=0, keepdims=True)
        vals.append(m)
        idxs.append(idx)
        logits = jnp.where(eid == idx, -jnp.inf, logits)
    ex = [jnp.exp(v - vals[0]) for v in vals]
    den = ex[0] + ex[1] + ex[2] + ex[3]
    e_ref[...] = jnp.concatenate(idxs, axis=0).astype(jnp.int32)
    g_ref[...] = jnp.concatenate([e / den for e in ex], axis=0)


def _router(x, w_router_t, b_router):
    t_pad, d = x.shape
    n_e = w_router_t.shape[0]
    tl = ROW_TILE
    return pl.pallas_call(
        _router_body,
        grid=(t_pad // tl,),
        in_specs=[
            pl.BlockSpec((tl, d), lambda i: (i, 0)),
            _resident((n_e, d), lambda i: (0, 0)),
            _resident((n_e, 1), lambda i: (0, 0)),
        ],
        out_specs=[pl.BlockSpec((TOP_K, tl), lambda i: (0, i))] * 2,
        out_shape=[jax.ShapeDtypeStruct((TOP_K, t_pad), jnp.int32), jax.ShapeDtypeStruct((TOP_K, t_pad), F32)],
        compiler_params=_params(1),
        name="router",
    )(x, w_router_t, b_router.reshape(n_e, 1))


def _route_plan(top_e, n_tok, n_e, n_items, t_pad):
    tk = n_tok * TOP_K
    e_flat = top_e[:, :n_tok].T.reshape(tk)
    onehot = (e_flat[:, None] == jnp.arange(n_e, dtype=jnp.int32)[None, :]).astype(jnp.int32)
    csum = jnp.cumsum(onehot, axis=0)
    rank = jnp.take_along_axis(csum, e_flat[:, None], axis=1)[:, 0] - 1
    counts = csum[-1]
    padded = (counts + SUBLANES - 1) // SUBLANES * SUBLANES
    seg_start = jnp.cumsum(padded) - padded
    dest = seg_start[e_flat] + rank
    n_rows = (tk + n_e * (SUBLANES - 1) + SUBLANES - 1) // SUBLANES * SUBLANES
    row_tok = jnp.zeros((n_rows,), jnp.int32).at[dest].set(jnp.arange(tk, dtype=jnp.int32) // TOP_K)
    items_per_e = (counts + MOE_CAP - 1) // MOE_CAP
    item_end = jnp.cumsum(items_per_e)
    total = item_end[-1]
    q = jnp.arange(n_items, dtype=jnp.int32)
    valid = q < total
    e_q = jnp.clip(jnp.searchsorted(item_end, q, side="right"), 0, n_e - 1).astype(jnp.int32)
    e_q = jnp.where(valid, e_q, e_q[jnp.maximum(total - 1, 0)])
    within = q - (item_end - items_per_e)[e_q]
    item_start = jnp.where(valid, seg_start[e_q] + within * MOE_CAP, 0).astype(jnp.int32)
    item_start = jnp.concatenate([item_start, jnp.sum(padded, keepdims=True).astype(jnp.int32)])
    item_n = jnp.where(valid, jnp.clip(counts[e_q] - within * MOE_CAP, 0, MOE_CAP), 0).astype(jnp.int32)
    pos = jnp.zeros((t_pad * TOP_K,), jnp.int32).at[:tk].set(dest.astype(jnp.int32))
    return e_q, item_start, item_n, row_tok, pos, n_rows


def _moe_body(ie_ref, ist_ref, in_ref, tok_ref,
              x_hbm, wg_ref, wu_ref, wd_ref, bg_ref, bu_ref, bd_ref, y_hbm,
              xstage, xb, acc, wgb, wub, wdb, zrows, sems, *, n_items, n_ff):
    q = pl.program_id(0)
    j = pl.program_id(1)
    n = in_ref[q]
    start = ist_ref[q]
    half = xstage.shape[1]
    tok_last = tok_ref.shape[0] - 1
    rows_per_body = MOE_CHUNK // n_ff

    def ceil8(v):
        return (v + SUBLANES - 1) // SUBLANES * SUBLANES

    def covered(item):
        return (in_ref[item] + MOE_CHUNK - 1) // MOE_CHUNK * MOE_CHUNK

    n8 = ceil8(n)
    nxt = jnp.minimum(q + 1, n_items - 1)
    nxt_base = ist_ref[nxt]
    nxt_rows = jnp.maximum(covered(q), ceil8(in_ref[nxt]))

    def gather_copy(r, tok):
        return pltpu.make_async_copy(x_hbm.at[pl.ds(tok, 1), :], xstage.at[pl.ds(r, 1), :], sems.at[0])

    def issue_rows(base, lo, hi):
        def body(gi, c):
            r0 = lo + gi * SUBLANES
            for u in range(SUBLANES):
                gather_copy(r0 + u, tok_ref[jnp.minimum(base + r0 + u, tok_last)]).start(priority=u % 2)
            return c

        lax.fori_loop(0, (hi - lo) // SUBLANES, body, 0)

    def wait_rows(count):
        def body(gi, c):
            for u in range(SUBLANES):
                gather_copy(gi * SUBLANES + u, 0).wait()
            return c

        lax.fori_loop(0, count // SUBLANES, body, 0)

    @pl.when((q == 0) & (j == 0))
    def _():
        xstage[...] = jnp.zeros(xstage.shape, jnp.uint32)
        zrows[...] = jnp.zeros(zrows.shape, F32)
        issue_rows(start, 0, n8)
        wait_rows(n8)

    @pl.when((q > 0) & (j == 0))
    def _():
        wait_rows(jnp.maximum(covered(jnp.maximum(q - 1, 0)), n8))

    @pl.when((j == 0) & (n > 0))
    def _():
        def cvt(c, carry):
            r0 = pl.multiple_of(c * MOE_CHUNK, MOE_CHUNK)
            w = xstage[pl.ds(r0, MOE_CHUNK), :]
            lo = pltpu.bitcast(lax.shift_left(w, jnp.uint32(16)), F32)
            hi = pltpu.bitcast(w & jnp.uint32(0xFFFF0000), F32)
            xb[pl.ds(r0, MOE_CHUNK), 0:half] = lo.astype(BF16)
            xb[pl.ds(r0, MOE_CHUNK), half:] = hi.astype(BF16)
            return carry

        lax.fori_loop(0, (n + MOE_CHUNK - 1) // MOE_CHUNK, cvt, 0)

    def out_copy(r0):
        return pltpu.make_async_copy(acc.at[pl.ds(pl.multiple_of(r0, SUBLANES), SUBLANES), :],
                                     y_hbm.at[pl.ds(pl.multiple_of(start + r0, SUBLANES), SUBLANES), :],
                                     sems.at[1])

    @pl.when(n > 0)
    def _():
        wgb[...] = wg_ref[...].astype(BF16)
        wub[...] = wu_ref[...].astype(BF16)
        wdb[...] = wd_ref[...].astype(BF16)

        chunks_per_step = covered(q) // MOE_CHUNK

        def chunk(r0, rows):
            slot0 = (j * chunks_per_step + lax.div(r0, jnp.int32(MOE_CHUNK))) * rows_per_body
            for u in range(rows_per_body):
                src = jnp.minimum(nxt_base + slot0 + u, tok_last)
                gather_copy(slot0 + u, tok_ref[src]).start(priority=u % 2)
            xc = xb[pl.ds(r0, rows), :]
            gate = jnp.dot(xc, wgb[...], preferred_element_type=F32) + bg_ref[...]
            lin = jnp.dot(xc, wub[...], preferred_element_type=F32) + bu_ref[...]
            glu = jnp.minimum(gate, SWIGLU_LIMIT)
            lin = jnp.clip(lin, -SWIGLU_LIMIT, SWIGLU_LIMIT)
            hid = glu * jax.nn.sigmoid(SWIGLU_ALPHA * glu) * (lin + 1.0)
            d = jnp.dot(hid.astype(BF16), wdb[...], preferred_element_type=F32)

            @pl.when(j == 0)
            def _():
                acc[pl.ds(r0, rows), :] = d + bd_ref[...]

            @pl.when(j > 0)
            def _():
                acc[pl.ds(r0, rows), :] += d

            @pl.when(j == n_ff - 1)
            def _():
                groups = jnp.minimum(n8 - r0, rows) // SUBLANES

                def put(gi, c2):
                    out_copy(r0 + gi * SUBLANES).start()
                    return c2

                lax.fori_loop(0, groups, put, 0)

        def full_chunk(c, carry):
            chunk(pl.multiple_of(c * MOE_CHUNK, MOE_CHUNK), MOE_CHUNK)
            return carry

        n_full = n // MOE_CHUNK
        rem = n - n_full * MOE_CHUNK
        lax.fori_loop(0, n_full, full_chunk, 0)
        r_rem = pl.multiple_of(n_full * MOE_CHUNK, MOE_CHUNK)

        @pl.when(rem > MOE_CHUNK // 2)
        def _():
            chunk(r_rem, MOE_CHUNK)

        @pl.when((rem > 0) & (rem <= MOE_CHUNK // 2))
        def _():
            chunk(r_rem, MOE_CHUNK // 2)

        @pl.when(j == n_ff - 1)
        def _():
            def wait_out(gi, c2):
                out_copy(gi * SUBLANES).wait()
                return c2

            lax.fori_loop(0, n8 // SUBLANES, wait_out, 0)

    @pl.when(j == n_ff - 1)
    def _():
        issue_rows(nxt_base, covered(q), nxt_rows)

    @pl.when((q == n_items - 1) & (j == n_ff - 1))
    def _():
        wait_rows(nxt_rows)
        used = ist_ref[n_items]
        n_tail = (y_hbm.shape[0] - used) // SUBLANES

        def tail_copy(gi):
            dst = pl.multiple_of(used + gi * SUBLANES, SUBLANES)
            return pltpu.make_async_copy(zrows, y_hbm.at[pl.ds(dst, SUBLANES), :], sems.at[1])

        def put(gi, c2):
            tail_copy(gi).start()
            return c2

        def wait(gi, c2):
            tail_copy(gi).wait()
            return c2

        lax.fori_loop(0, n_tail, put, 0)
        lax.fori_loop(0, n_tail, wait, 0)


def _moe_experts(xp, plan, w_gu, b_gu, w_down, b_down, layer, n_items):
    item_e, item_start, item_n, row_tok, _, n_rows = plan
    d = 2 * xp.shape[1]
    d_ff = w_down.shape[2]
    n_e = w_gu.shape[1]
    n_ff = d_ff // FF_TILE
    assert d_ff % FF_TILE == 0 and MOE_CAP % MOE_CHUNK == 0 and MOE_CHUNK % n_ff == 0

    def ff_tile(q, j, inn):
        return jnp.where(inn[q] > 0, j, n_ff - 1)

    bg4 = b_gu.reshape(b_gu.shape[0], n_e, 1, 2 * d_ff)
    bd4 = b_down.reshape(b_down.shape[0], n_e, 1, d)
    grid_spec = pltpu.PrefetchScalarGridSpec(
        num_scalar_prefetch=4,
        grid=(n_items, n_ff),
        in_specs=[
            pl.BlockSpec(memory_space=pl.ANY),
            pl.BlockSpec((None, None, d, FF_TILE), lambda q, j, ie, ist, inn, tok: (layer, ie[q], 0, ff_tile(q, j, inn))),
            pl.BlockSpec((None, None, d, FF_TILE),
                         lambda q, j, ie, ist, inn, tok: (layer, ie[q], 0, n_ff + ff_tile(q, j, inn))),
            pl.BlockSpec((None, None, FF_TILE, d), lambda q, j, ie, ist, inn, tok: (layer, ie[q], ff_tile(q, j, inn), 0)),
            pl.BlockSpec((None, None, 1, FF_TILE), lambda q, j, ie, ist, inn, tok: (layer, ie[q], 0, ff_tile(q, j, inn))),
            pl.BlockSpec((None, None, 1, FF_TILE),
                         lambda q, j, ie, ist, inn, tok: (layer, ie[q], 0, n_ff + ff_tile(q, j, inn))),
            pl.BlockSpec((None, None, 1, d), lambda q, j, ie, ist, inn, tok: (layer, ie[q], 0, 0)),
        ],
        out_specs=pl.BlockSpec(memory_space=pl.ANY),
        scratch_shapes=[
            pltpu.VMEM((MOE_CAP, d // 2), jnp.uint32),
            pltpu.VMEM((MOE_CAP, d), BF16),
            pltpu.VMEM((MOE_CAP, d), F32),
            pltpu.VMEM((d, FF_TILE), BF16),
            pltpu.VMEM((d, FF_TILE), BF16),
            pltpu.VMEM((FF_TILE, d), BF16),
            pltpu.VMEM((SUBLANES, d), F32),
            pltpu.SemaphoreType.DMA((2,)),
        ],
    )
    return pl.pallas_call(
        functools.partial(_moe_body, n_items=n_items, n_ff=n_ff),
        grid_spec=grid_spec,
        out_shape=jax.ShapeDtypeStruct((n_rows, d), F32),
        compiler_params=_params(2, MOE_VMEM_LIMIT),
        name="moe_experts",
    )(item_e, item_start, item_n, row_tok, xp, w_gu, w_gu, w_down, bg4, bg4, bd4)


def _combine_body(pos_ref, y_hbm, gate_ref, x_ref, g_ref, b_ref, o_ref, ybuf, sem, *, alpha, n_tok):
    s = pl.program_id(0)
    tm = x_ref.shape[0]
    n_valid = jnp.clip(n_tok - s * tm, 0, tm)

    def copy(r, k, src_row):
        return pltpu.make_async_copy(y_hbm.at[pl.ds(src_row, 1), :], ybuf.at[k, pl.ds(r, 1), :], sem.at[0])

    @pl.when(s == 0)
    def _():
        ybuf[...] = jnp.zeros(ybuf.shape, F32)

    def issue(r, c):
        base = (s * tm + r) * TOP_K
        for k in range(TOP_K):
            copy(r, k, pos_ref[base + k]).start(priority=k % 2)
        return c

    lax.fori_loop(0, n_valid, issue, 0)

    def wait(r, c):
        for k in range(TOP_K):
            copy(r, k, 0).wait()
        return c

    lax.fori_loop(0, n_valid, wait, 0)
    gates = gate_ref[...]
    y = gates[:, 0:1] * ybuf[0]
    for k in range(1, TOP_K):
        y = y + gates[:, k:k + 1] * ybuf[k]
    o_ref[...] = _layer_norm(alpha * x_ref[...] + y, g_ref[...], b_ref[...])


def _combine_ln(yr, pos, gates_tk, x, g, b, *, alpha, n_tok):
    t_pad, d = x.shape
    tm = COMBINE_TILE
    grid_spec = pltpu.PrefetchScalarGridSpec(
        num_scalar_prefetch=1,
        grid=(t_pad // tm,),
        in_specs=[
            pl.BlockSpec(memory_space=pl.ANY),
            pl.BlockSpec((tm, TOP_K), lambda s, p: (s, 0)),
            pl.BlockSpec((tm, d), lambda s, p: (s, 0)),
            _resident((1, d), lambda s, p: (0, 0)),
            _resident((1, d), lambda s, p: (0, 0)),
        ],
        out_specs=pl.BlockSpec((tm, d), lambda s, p: (s, 0)),
        scratch_shapes=[pltpu.VMEM((TOP_K, tm, d), F32), pltpu.SemaphoreType.DMA((1,))],
    )
    return pl.pallas_call(
        functools.partial(_combine_body, alpha=alpha, n_tok=n_tok),
        grid_spec=grid_spec,
        out_shape=jax.ShapeDtypeStruct((t_pad, d), F32),
        compiler_params=_params(1),
        name="combine_ln",
    )(pos, yr, gates_tk, x, g.reshape(1, d), b.reshape(1, d))


def _top_blocks_t(gate, row, n_valid_rows, n_rows):
    sel = jnp.zeros(gate.shape, F32)
    g = gate
    for _ in range(MOBA_TOPK):
        m = jnp.max(g, axis=0, keepdims=True)
        idx = jnp.min(jnp.where(g == m, row, float(n_rows)), axis=0, keepdims=True)
        pick = row == idx
        sel = jnp.where(pick, jnp.where(row < n_valid_rows, 1.0, sel), sel)
        g = jnp.where(pick, -jnp.inf, g)
    return sel


def _moba_body(q_ref, k_ref, v_ref, o_ref, kb, vt, kmean, sel_s, *, n_blocks):
    i = pl.program_id(2)
    blk = MOBA_BLOCK
    scale = HEAD_DIM ** -0.5

    @pl.when(i == 0)
    def _():
        kb[...] = k_ref[...].astype(BF16)
        for jb in range(n_blocks):
            rows = slice(jb * blk, (jb + 1) * blk)
            kmean[jb:jb + 1, :] = jnp.mean(k_ref[rows, :], axis=0, keepdims=True)
            vt[:, rows] = v_ref[rows, :].T.astype(BF16)

    qt = q_ref[...].T
    row = lax.broadcasted_iota(jnp.int32, (n_blocks, blk), 0).astype(F32)
    gate = jnp.dot(kmean[...], qt, preferred_element_type=F32, precision=lax.Precision.HIGHEST)
    own_f = i.astype(F32)
    gate = jnp.where(row < own_f, gate, NEG_INF)
    sel_s[...] = _top_blocks_t(gate, row, own_f, n_blocks)

    qtb = qt.astype(BF16)
    own = pl.multiple_of(i * blk, blk)
    st = jnp.dot(kb[pl.ds(own, blk), :], qtb, preferred_element_type=F32) * scale
    kk = lax.broadcasted_iota(jnp.int32, (blk, blk), 0)
    qq = lax.broadcasted_iota(jnp.int32, (blk, blk), 1)
    st = jnp.where(kk <= qq, st, NEG_INF)
    m0 = jnp.max(st, axis=0, keepdims=True)
    p = jnp.exp(st - m0)
    l0 = jnp.sum(p, axis=0, keepdims=True)
    a0 = jnp.dot(vt[:, pl.ds(own, blk)], p.astype(BF16), preferred_element_type=F32)

    def past(t, carry):
        m, l, a = carry
        starts, scores = [], []
        m_new = m
        for u in range(MOBA_GROUP):
            jb = jnp.minimum(t * MOBA_GROUP + u, n_blocks - 1)
            r0 = pl.multiple_of(jb * blk, blk)
            s_j = jnp.dot(kb[pl.ds(r0, blk), :], qtb, preferred_element_type=F32) * scale
            s_j = jnp.where(sel_s[pl.ds(jb, 1), :] > 0.0, s_j, NEG_INF)
            m_new = jnp.maximum(m_new, jnp.max(s_j, axis=0, keepdims=True))
            starts.append(r0)
            scores.append(s_j)
        corr = jnp.exp(m - m_new)
        l = corr * l
        a = corr * a
        for r0, s_j in zip(starts, scores):
            p_j = jnp.exp(s_j - m_new)
            l = l + jnp.sum(p_j, axis=0, keepdims=True)
            a = a + jnp.dot(vt[:, pl.ds(r0, blk)], p_j.astype(BF16), preferred_element_type=F32)
        return m_new, l, a

    n_trips = (i + MOBA_GROUP - 1) // MOBA_GROUP
    _, l_fin, a_fin = lax.fori_loop(0, n_trips, past, (m0, l0, a0))
    o_ref[...] = (a_fin / l_fin).T.astype(BF16)


def _moba_prompt(hq, kf, vf, *, n_seq, seq_len, n_heads):
    blk = MOBA_BLOCK
    n_blocks = seq_len // blk
    assert seq_len % blk == 0 and n_blocks >= MOBA_TOPK
    return pl.pallas_call(
        functools.partial(_moba_body, n_blocks=n_blocks),
        grid=(n_seq, n_heads, n_blocks),
        in_specs=[
            pl.BlockSpec((blk, HEAD_DIM), lambda b, h, i: (b * n_blocks + i, h)),
            pl.BlockSpec((seq_len, HEAD_DIM), lambda b, h, i: (b, h)),
            pl.BlockSpec((seq_len, HEAD_DIM), lambda b, h, i: (b, h)),
        ],
        out_specs=pl.BlockSpec((blk, HEAD_DIM), lambda b, h, i: (b * n_blocks + i, h)),
        out_shape=jax.ShapeDtypeStruct((n_seq * seq_len, n_heads * HEAD_DIM), BF16),
        scratch_shapes=[
            pltpu.VMEM((seq_len, HEAD_DIM), BF16),
            pltpu.VMEM((HEAD_DIM, seq_len), BF16),
            pltpu.VMEM((n_blocks, HEAD_DIM), F32),
            pltpu.VMEM((n_blocks, blk), F32),
        ],
        compiler_params=_params(3),
        name="moba_prompt",
    )(hq, kf, vf)


def _page_means_body(pt_ref, *refs, pages_per_block):
    o_ref = refs[-1]
    j = pl.program_id(1)
    n_heads = o_ref.shape[0]
    for h in range(n_heads):
        tot = jnp.sum(refs[0][h], axis=0, keepdims=True)
        for r in refs[1:pages_per_block]:
            tot = tot + jnp.sum(r[h], axis=0, keepdims=True)
        o_ref[h, pl.ds(j, 1), :] = tot * (1.0 / MOBA_BLOCK)


def _page_means(cache_k_hm, page_table_flat, *, db, n_pages):
    _, n_heads, page, hd = cache_k_hm.shape
    ppb = MOBA_BLOCK // page
    n_blocks = n_pages // ppb

    def page_spec(p):
        return pl.BlockSpec((None, n_heads, page, hd), lambda b, j, pt: (pt[b * n_pages + j * ppb + p], 0, 0, 0))

    grid_spec = pltpu.PrefetchScalarGridSpec(
        num_scalar_prefetch=1,
        grid=(db, n_blocks),
        in_specs=[page_spec(p) for p in range(ppb)],
        out_specs=pl.BlockSpec((None, n_heads, n_blocks, hd), lambda b, j, pt: (b, 0, 0, 0)),
    )
    return pl.pallas_call(
        functools.partial(_page_means_body, pages_per_block=ppb),
        grid_spec=grid_spec,
        out_shape=jax.ShapeDtypeStruct((db, n_heads, n_blocks, hd), F32),
        compiler_params=_params(2),
        name="page_means",
    )(page_table_flat, *([cache_k_hm] * ppb))


def _select_body(q_ref, km_ref, o_ref, *, n_heads):
    b = pl.program_id(0)
    n_blocks = km_ref.shape[1]
    qrow = q_ref[pl.ds(b, 1), :]
    col = lax.broadcasted_iota(jnp.int32, (SUBLANES, n_blocks), 1).astype(F32)
    lane = lax.broadcasted_iota(jnp.int32, (1, LANES), 1)
    for h in range(n_heads):
        hs = slice(h * HEAD_DIM, (h + 1) * HEAD_DIM)
        q8 = jnp.broadcast_to(qrow[:, hs], (SUBLANES, HEAD_DIM))
        g = _dot_nt(q8, km_ref[h], precision=lax.Precision.HIGHEST)
        out = jnp.zeros((1, LANES), jnp.int32)
        for r in range(MOBA_TOPK):
            m = jnp.max(g, axis=1, keepdims=True)
            idx = jnp.min(jnp.where(g == m, col, float(n_blocks)), axis=1, keepdims=True)
            g = jnp.where(col == idx, -jnp.inf, g)
            out = jnp.where(lane == r, idx[0:1, :].astype(jnp.int32), out)
        o_ref[0, h:h + 1, :] = out


def _moba_select(hq, kmean_hs, *, row0, db, n_heads):
    n_blocks = kmean_hs.shape[2]
    w = n_heads * HEAD_DIM
    assert n_blocks >= MOBA_TOPK and row0 % db == 0
    return pl.pallas_call(
        functools.partial(_select_body, n_heads=n_heads),
        grid=(db,),
        in_specs=[
            pl.BlockSpec((db, w), lambda b: (row0 // db, 0)),
            pl.BlockSpec((None, n_heads, n_blocks, HEAD_DIM), lambda b: (b, 0, 0, 0)),
        ],
        out_specs=pl.BlockSpec((1, n_heads, LANES), lambda b: (b, 0, 0)),
        out_shape=jax.ShapeDtypeStruct((db, n_heads, LANES), jnp.int32),
        compiler_params=_params(1),
        name="moba_select",
    )(hq, kmean_hs)


def _decode_attn_body(pg_ref, q_ref, kn_ref, vn_ref, k_hbm, v_hbm, o_ref, kbuf, vbuf, sems,
                      *, n_heads, n_sel, n_steps):
    g = pl.program_id(0)
    page = kbuf.shape[2]
    scale = HEAD_DIM ** -0.5

    def copies(step, slot):
        h = step % n_heads
        out = []
        for s in range(n_sel):
            pg = pg_ref[step * n_sel + s]
            out.append(pltpu.make_async_copy(k_hbm.at[pg, h], kbuf.at[slot, s], sems.at[0, slot]))
            out.append(pltpu.make_async_copy(v_hbm.at[pg, h], vbuf.at[slot, s], sems.at[1, slot]))
        return out

    slot = g % 2

    @pl.when(g == 0)
    def _():
        for c in copies(0, 0):
            c.start()

    @pl.when(g + 1 < n_steps)
    def _():
        for c in copies(g + 1, 1 - slot):
            c.start()

    for c in copies(g, slot):
        c.wait()

    q = q_ref[...]
    q8 = jnp.broadcast_to(q, (SUBLANES, HEAD_DIM)).astype(BF16)
    k = kbuf[slot].reshape(n_sel * page, HEAD_DIM).astype(BF16)
    v = vbuf[slot].reshape(n_sel * page, HEAD_DIM).astype(BF16)
    sc = _dot_nt(q8, k) * scale
    s_own = jnp.sum(q * kn_ref[...], axis=-1, keepdims=True) * scale
    m = jnp.maximum(jnp.max(sc, axis=-1, keepdims=True), s_own)
    p = jnp.exp(sc - m)
    p_own = jnp.exp(s_own - m)
    den = jnp.sum(p, axis=-1, keepdims=True) + p_own
    acc = jnp.dot(p.astype(BF16), v, preferred_element_type=F32) + p_own * vn_ref[...]
    o_ref[...] = (acc / den)[0:1, :]


def _moba_decode(pages, q4, kn4, vn4, cache_k, cache_v, *, db, n_heads, n_sel):
    page = cache_k.shape[2]
    n_steps = db * n_heads
    one = pl.BlockSpec((None, None, 1, HEAD_DIM), lambda g, pg: (g // n_heads, g % n_heads, 0, 0))
    grid_spec = pltpu.PrefetchScalarGridSpec(
        num_scalar_prefetch=1,
        grid=(n_steps,),
        in_specs=[one, one, one, pl.BlockSpec(memory_space=pl.ANY), pl.BlockSpec(memory_space=pl.ANY)],
        out_specs=one,
        scratch_shapes=[pltpu.VMEM((2, n_sel, page, HEAD_DIM), F32), pltpu.VMEM((2, n_sel, page, HEAD_DIM), F32),
                        pltpu.SemaphoreType.DMA((2, 2))],
    )
    return pl.pallas_call(
        functools.partial(_decode_attn_body, n_heads=n_heads, n_sel=n_sel, n_steps=n_steps),
        grid_spec=grid_spec,
        out_shape=jax.ShapeDtypeStruct((db, n_heads, 1, HEAD_DIM), F32),
        compiler_params=_params(1),
        name="moba_decode",
    )(pages, q4, kn4, vn4, cache_k, cache_v)


def kernel(x_prompt, x_sample, cache_conv, cache_k, cache_v, cache_mem_k, cache_mem_v, page_table, mem_prompt,
           w_in_a, w_dw, b_dw, conv_ln_g, conv_ln_b, w_out_a, w_in_b, w_out_b, w_kv_shared, w_mem_kv,
           ln1_g, ln1_b, ln2_g, ln2_b, w_router, b_router, w_gu, b_gu, w_down, b_down):
    n_seq, seq_len, d = x_prompt.shape
    db, dseq, _ = x_sample.shape
    depth = ln1_g.shape[0]
    n_a = w_in_a.shape[0]
    n_e = w_router.shape[2]
    ch = w_dw.shape[2]
    moba_w = w_in_b.shape[2] - MEM_W
    n_heads = moba_w // HEAD_DIM
    n_mem = mem_prompt.shape[1]
    n_pool, page = cache_k.shape[0], cache_k.shape[1]
    n_pages = page_table.shape[1]
    past_len = n_pages * page
    alpha = (2 * depth) ** 0.25
    tl = ROW_TILE
    assert dseq == 1 and db % SUBLANES == 0 and db <= tl and seq_len % tl == 0
    assert ch + MEM_W == d and moba_w == ch and MOBA_BLOCK % page == 0 and past_len % MOBA_BLOCK == 0
    n_prompt = n_seq * seq_len
    n_tok = n_prompt + db
    t_pad = n_prompt + tl
    ppb = MOBA_BLOCK // page
    n_items = n_e + (n_tok * TOP_K) // MOE_CAP

    x = jnp.concatenate([x_prompt.reshape(n_prompt, d), x_sample.reshape(db, d),
                         jnp.zeros((tl - db, d), F32)], axis=0)

    pos = jnp.concatenate([jnp.tile(jnp.arange(seq_len, dtype=jnp.int32), n_seq),
                           jnp.full((db,), past_len, jnp.int32), jnp.zeros((tl - db,), jnp.int32)])
    half = HEAD_DIM // 2
    inv = ROPE_THETA ** (-jnp.arange(half, dtype=F32) / half)
    ang = pos.astype(F32)[:, None] * inv[None, :]
    rope = (jnp.concatenate([jnp.cos(ang), jnp.cos(ang)], axis=1),
            jnp.concatenate([-jnp.sin(ang), jnp.sin(ang)], axis=1))

    mem2 = mem_prompt.reshape(n_seq * n_mem, d)
    mkv = [_dense(mem2, w_mem_kv[l].astype(BF16), n_out=2, name="mem_kv") for l in range(depth)]
    new_mem_k = jnp.stack([kv[0] for kv in mkv]).reshape(depth, n_seq, n_mem, MEM_W)
    new_mem_v = jnp.stack([kv[1] for kv in mkv]).reshape(depth, n_seq, n_mem, MEM_W)
    mem_k_s = cache_mem_k.reshape(depth, db, n_mem, MEM_W)
    mem_v_s = cache_mem_v.reshape(depth, db, n_mem, MEM_W)

    pt_flat = page_table.reshape(-1).astype(jnp.int32)
    cache_k_hm = jnp.swapaxes(cache_k, 1, 2)
    cache_v_hm = jnp.swapaxes(cache_v, 1, 2)

    conv_tails, conv_new = [], []
    kf = vf = kmean_s = None
    for l in range(depth):
        if l < n_a:
            h = _dense(x, w_in_a[l].astype(BF16), name="in_proj_a")
            c_act, tail, us = _conv_module(h, jnp.swapaxes(cache_conv[l], 0, 1), w_dw[l], b_dw[l],
                                           conv_ln_g[l], conv_ln_b[l], n_seq=n_seq, seq_len=seq_len, ch=ch)
            conv_tails.append(tail[:, CONV_HALO - (CONV_W - 1):])
            conv_new.append(jnp.concatenate([cache_conv[l][:, 1:], us[:, None, :]], axis=1))
            m_act = _mem_attention(h, 2 * ch // MEM_W, new_mem_k[l], new_mem_v[l], mem_k_s[l], mem_v_s[l],
                                   n_seq=n_seq, seq_len=seq_len)
            a1, a1s, w_out = c_act, None, w_out_a[l]
        else:
            jb = l - n_a
            if kf is None:
                kf, vf = _dense(x, w_kv_shared.astype(BF16), rope=rope, n_rope_heads=n_heads, n_out=2,
                                name="kv_proj")
                kmean_s = _page_means(cache_k_hm, pt_flat, db=db, n_pages=n_pages)
            hq = _dense(x, w_in_b[jb].astype(BF16), rope=rope, n_rope_heads=n_heads, name="in_proj_b")
            o_prompt = _moba_prompt(hq, kf, vf, n_seq=n_seq, seq_len=seq_len, n_heads=n_heads)
            sel = _moba_select(hq, kmean_s, row0=n_prompt, db=db, n_heads=n_heads)[:, :, :MOBA_TOPK]
            blk_pages = sel[..., None] * ppb + jnp.arange(ppb, dtype=jnp.int32)
            pages = jnp.take_along_axis(page_table.astype(jnp.int32)[:, None, :],
                                        blk_pages.reshape(db, n_heads, MOBA_TOPK * ppb), axis=2)
            four = lambda a: a[n_prompt:n_tok].reshape(db, n_heads, 1, HEAD_DIM)
            o_s = _moba_decode(pages.reshape(-1), four(hq[:, :moba_w]), four(kf), four(vf), cache_k_hm, cache_v_hm,
                               db=db, n_heads=n_heads, n_sel=MOBA_TOPK * ppb)
            a1s = jnp.concatenate([o_s.reshape(db, moba_w), jnp.zeros((tl - db, moba_w), F32)], axis=0).astype(BF16)
            m_act = _mem_attention(hq, moba_w // MEM_W, new_mem_k[l], new_mem_v[l], mem_k_s[l], mem_v_s[l],
                                   n_seq=n_seq, seq_len=seq_len)
            a1, w_out = o_prompt, w_out_b[jb]
        x, xp = _outproj_ln(a1, a1s, m_act, w_out.astype(BF16), x, ln1_g[l], ln1_b[l], alpha=alpha)

        top_e, gates = _router(x, w_router[l].T, b_router[l])
        plan = _route_plan(top_e, n_tok, n_e, n_items, t_pad)
        yr = _moe_experts(xp, plan, w_gu, b_gu, w_down, b_down, l, n_items)
        x = _combine_ln(yr, plan[4], gates.T, x, ln2_g[l], ln2_b[l], alpha=alpha, n_tok=n_tok)

    y_prompt = x[:n_prompt].reshape(n_seq, seq_len, d)
    y_sample = x[n_prompt:n_tok].reshape(db, 1, d)
    heads4 = lambda a, n, s: a.reshape(n, s, n_heads, HEAD_DIM)
    mem5 = lambda a: a.reshape(depth, n_seq, n_mem, N_MEM_HEADS, HEAD_DIM)
    return (y_prompt, y_sample, jnp.stack(conv_tails), jnp.stack(conv_new),
            heads4(kf[:n_prompt], n_seq, seq_len), heads4(vf[:n_prompt], n_seq, seq_len),
            heads4(kf[n_prompt:n_tok], db, 1), heads4(vf[n_prompt:n_tok], db, 1),
            mem5(new_mem_k), mem5(new_mem_v))
```

```python
import functools

import jax
import jax.numpy as jnp
from jax import lax
from jax.experimental import pallas as pl
from jax.experimental.pallas import tpu as pltpu

F32 = jnp.float32
BF16 = jnp.bfloat16

HEAD_DIM = 128
N_MEM_HEADS = 4
MEM_W = N_MEM_HEADS * HEAD_DIM
CONV_W = 31
MOBA_BLOCK = 256
MOBA_TOPK = 3
ROPE_THETA = 10000.0
TOP_K = 4
SWIGLU_LIMIT = 7.0
SWIGLU_ALPHA = 1.702
LN_EPS = 1e-5
NEG_INF = -1e30

ROW_TILE = MOBA_BLOCK
CONV_HALO = 32
SUBLANES = 8
LANES = 128
FF_TILE = 512
MOE_CHUNK = 256
MOE_CAP = 1280
COMBINE_TILE = 128
MOBA_GROUP = 4
PAGE_MEAN_BLOCKS = 4
VMEM_LIMIT = 48 * 1024 * 1024
MOE_VMEM_LIMIT = 60 * 1024 * 1024


def _params(n_axes, vmem=VMEM_LIMIT):
    return pltpu.CompilerParams(dimension_semantics=("arbitrary",) * n_axes, vmem_limit_bytes=vmem)


def _resident(shape, index_map):
    return pl.BlockSpec(shape, index_map, pipeline_mode=pl.Buffered(1))


def _layer_norm(z, g, b):
    mu = jnp.mean(z, axis=-1, keepdims=True)
    zc = z - mu
    var = jnp.mean(zc * zc, axis=-1, keepdims=True)
    return zc * lax.rsqrt(var + LN_EPS) * g + b


def _dot_nt(a, b, **kw):
    return lax.dot_general(a, b, (((1,), (1,)), ((), ())), preferred_element_type=F32, **kw)


def _dense_body(x_ref, w_ref, *rest, n_rope_heads, col_chunk, n_out):
    o_refs = rest[-n_out:]
    if n_rope_heads:
        cos_ref, sin_ref = rest[0], rest[1]
    x = x_ref[...].astype(BF16)
    n = w_ref.shape[1]
    out_w = o_refs[0].shape[1]
    for c0 in range(0, n, col_chunk):
        cw = min(col_chunk, n - c0)
        y = jnp.dot(x, w_ref[:, c0:c0 + cw], preferred_element_type=F32)
        for h0 in range(0, cw, HEAD_DIM):
            blk = y[:, h0:h0 + HEAD_DIM]
            col = c0 + h0
            if col // HEAD_DIM < n_rope_heads:
                blk = blk * cos_ref[...] + pltpu.roll(blk, HEAD_DIM // 2, 1) * sin_ref[...]
            o_refs[col // out_w][:, col % out_w:col % out_w + HEAD_DIM] = blk


def _dense(x, w, rope=None, n_rope_heads=0, n_out=1, name="dense"):
    m, k = x.shape
    n = w.shape[1]
    tm = min(ROW_TILE, m)
    assert m % tm == 0 and n % (n_out * HEAD_DIM) == 0
    in_specs = [pl.BlockSpec((tm, k), lambda i: (i, 0)), _resident((k, n), lambda i: (0, 0))]
    args = [x, w]
    if n_rope_heads:
        in_specs += [pl.BlockSpec((tm, HEAD_DIM), lambda i: (i, 0))] * 2
        args += list(rope)
    outs = pl.pallas_call(
        functools.partial(_dense_body, n_rope_heads=n_rope_heads, col_chunk=512, n_out=n_out),
        grid=(m // tm,),
        in_specs=in_specs,
        out_specs=[pl.BlockSpec((tm, n // n_out), lambda i: (i, 0))] * n_out,
        out_shape=[jax.ShapeDtypeStruct((m, n // n_out), F32)] * n_out,
        compiler_params=_params(1),
        name=name,
    )(*args)
    return outs[0] if n_out == 1 else outs


def _conv_body(a_ref, g_ref, prev_ref, w_ref, b_ref, cg_ref, cb_ref, c_ref, tail_ref, us_ref, uext, ybuf,
               *, tiles_per_seq, n_prompt_tiles, db):
    s = pl.program_id(0)
    i = s % tiles_per_seq
    tl, ch = a_ref.shape
    u = a_ref[...] * jax.nn.sigmoid(g_ref[...])

    def ln_swish(y):
        z = _layer_norm(y, cg_ref[...], cb_ref[...])
        return z * jax.nn.sigmoid(z)

    @pl.when(s < n_prompt_tiles)
    def _():
        @pl.when(i == 0)
        def _():
            uext[0:CONV_HALO, :] = jnp.zeros((CONV_HALO, ch), F32)

        uext[CONV_HALO:CONV_HALO + tl, :] = u

        def lane_chunk(cc, carry):
            c0 = pl.multiple_of(cc * LANES, LANES)
            acc = jnp.zeros((tl, LANES), F32) + b_ref[:, pl.ds(c0, LANES)]
            shifted = [uext[p:p + tl + CONV_HALO - (SUBLANES if p else 0), pl.ds(c0, LANES)]
                       for p in range(SUBLANES)]
            for k in range(CONV_W):
                off = k + CONV_HALO - (CONV_W - 1)
                a, p = divmod(off, SUBLANES)
                acc = acc + w_ref[k:k + 1, pl.ds(c0, LANES)] * shifted[p][a * SUBLANES:a * SUBLANES + tl]
            ybuf[:, pl.ds(c0, LANES)] = acc
            return carry

        lax.fori_loop(0, ch // LANES, lane_chunk, 0)
        c_ref[...] = ln_swish(ybuf[...]).astype(BF16)

        @pl.when(i == tiles_per_seq - 1)
        def _():
            tail_ref[0] = uext[tl:tl + CONV_HALO, :]

        uext[0:CONV_HALO, :] = uext[tl:tl + CONV_HALO, :]

    @pl.when(s == n_prompt_tiles)
    def _():
        us = u[0:db, :]
        acc = b_ref[...] + w_ref[CONV_W - 1:CONV_W, :] * us
        for k in range(CONV_W - 1):
            acc = acc + w_ref[k:k + 1, :] * prev_ref[k]
        y = ln_swish(acc)
        c_ref[...] = jnp.concatenate([y, jnp.zeros((tl - db, ch), F32)], axis=0).astype(BF16)
        us_ref[...] = us


def _conv_module(h, prev_t, w_dw, b_dw, cg, cb, *, n_seq, seq_len, ch):
    t_pad = h.shape[0]
    tl = ROW_TILE
    tiles_per_seq = seq_len // tl
    n_prompt_tiles = n_seq * tiles_per_seq
    db = prev_t.shape[1]
    row = lambda v: v.reshape(1, ch)
    return pl.pallas_call(
        functools.partial(_conv_body, tiles_per_seq=tiles_per_seq, n_prompt_tiles=n_prompt_tiles, db=db),
        grid=(n_prompt_tiles + 1,),
        in_specs=[
            pl.BlockSpec((tl, ch), lambda s: (s, 0)),
            pl.BlockSpec((tl, ch), lambda s: (s, 1)),
            _resident((CONV_W - 1, db, ch), lambda s: (0, 0, 0)),
            _resident((CONV_W, ch), lambda s: (0, 0)),
            _resident((1, ch), lambda s: (0, 0)),
            _resident((1, ch), lambda s: (0, 0)),
            _resident((1, ch), lambda s: (0, 0)),
        ],
        out_specs=[
            pl.BlockSpec((tl, ch), lambda s: (s, 0)),
            pl.BlockSpec((1, CONV_HALO, ch), lambda s: (jnp.minimum(s // tiles_per_seq, n_seq - 1), 0, 0)),
            pl.BlockSpec((db, ch), lambda s: (0, 0)),
        ],
        out_shape=[
            jax.ShapeDtypeStruct((t_pad, ch), BF16),
            jax.ShapeDtypeStruct((n_seq, CONV_HALO, ch), F32),
            jax.ShapeDtypeStruct((db, ch), F32),
        ],
        scratch_shapes=[pltpu.VMEM((CONV_HALO + tl, ch), F32), pltpu.VMEM((tl, ch), F32)],
        compiler_params=_params(1),
        name="conv_module",
    )(h, h, prev_t, w_dw, row(b_dw), row(cg), row(cb))


def _attend_full(q, k, v):
    sc = _dot_nt(q, k) * (HEAD_DIM ** -0.5)
    p = jnp.exp(sc - jnp.max(sc, axis=-1, keepdims=True))
    o = jnp.dot(p.astype(BF16), v, preferred_element_type=F32)
    return o / jnp.sum(p, axis=-1, keepdims=True)


def _memattn_body(q_ref, mkp_ref, mvp_ref, mks_ref, mvs_ref, o_ref, *, n_prompt_tiles, db):
    s = pl.program_id(0)
    tl = q_ref.shape[0]
    heads = [slice(h * HEAD_DIM, (h + 1) * HEAD_DIM) for h in range(N_MEM_HEADS)]

    @pl.when(s < n_prompt_tiles)
    def _():
        q = q_ref[...].astype(BF16)
        k = mkp_ref[0].astype(BF16)
        v = mvp_ref[0].astype(BF16)
        for hs in heads:
            o_ref[:, hs] = _attend_full(q[:, hs], k[:, hs], v[:, hs]).astype(BF16)

    @pl.when(s == n_prompt_tiles)
    def _():
        q = q_ref[0:db, :].astype(BF16)
        row = lax.broadcasted_iota(jnp.int32, (db, HEAD_DIM), 0)
        outs = []
        for hs in heads:
            acc = jnp.zeros((db, HEAD_DIM), F32)
            for b in range(db):
                o = _attend_full(q[:, hs], mks_ref[b][:, hs].astype(BF16), mvs_ref[b][:, hs].astype(BF16))
                acc = jnp.where(row == b, o, acc)
            outs.append(acc)
        full = jnp.concatenate(outs, axis=1)
        o_ref[...] = jnp.concatenate([full, jnp.zeros((tl - db, MEM_W), F32)], axis=0).astype(BF16)


def _mem_attention(h, q_col_block, mkp, mvp, mks, mvs, *, n_seq, seq_len):
    t_pad = h.shape[0]
    tl = ROW_TILE
    tiles_per_seq = seq_len // tl
    n_prompt_tiles = n_seq * tiles_per_seq
    db, n_mem = mks.shape[0], mks.shape[1]
    seq_of = lambda s: (jnp.minimum(s // tiles_per_seq, n_seq - 1), 0, 0)
    return pl.pallas_call(
        functools.partial(_memattn_body, n_prompt_tiles=n_prompt_tiles, db=db),
        grid=(n_prompt_tiles + 1,),
        in_specs=[
            pl.BlockSpec((tl, MEM_W), lambda s: (s, q_col_block)),
            pl.BlockSpec((1, n_mem, MEM_W), seq_of),
            pl.BlockSpec((1, n_mem, MEM_W), seq_of),
            _resident((db, n_mem, MEM_W), lambda s: (0, 0, 0)),
            _resident((db, n_mem, MEM_W), lambda s: (0, 0, 0)),
        ],
        out_specs=pl.BlockSpec((tl, MEM_W), lambda s: (s, 0)),
        out_shape=jax.ShapeDtypeStruct((t_pad, MEM_W), BF16),
        compiler_params=_params(1),
        name="mem_attention",
    )(h, mkp, mvp, mks, mvs)


def _outproj_body(*refs, alpha, n_prompt_tiles, has_decode_tile):
    if has_decode_tile:
        a1_ref, a1s_ref, a2_ref, w_ref, x_ref, g_ref, b_ref, o_ref, op_ref = refs
    else:
        a1_ref, a2_ref, w_ref, x_ref, g_ref, b_ref, o_ref, op_ref = refs
    s = pl.program_id(0)
    k1 = a1_ref.shape[1]
    half = op_ref.shape[1]

    def finish(a1):
        y = jnp.dot(a1, w_ref[0:k1, :], preferred_element_type=F32)
        y = y + jnp.dot(a2_ref[...], w_ref[k1:, :], preferred_element_type=F32)
        xn = _layer_norm(alpha * x_ref[...] + y, g_ref[...], b_ref[...])
        o_ref[...] = xn
        bits = pltpu.bitcast(xn.astype(BF16).astype(F32), jnp.uint32)
        op_ref[...] = lax.shift_right_logical(bits[:, 0:half], jnp.uint32(16)) | (bits[:, half:] & jnp.uint32(0xFFFF0000))

    if has_decode_tile:
        @pl.when(s < n_prompt_tiles)
        def _():
            finish(a1_ref[...])

        @pl.when(s == n_prompt_tiles)
        def _():
            finish(a1s_ref[...])
    else:
        finish(a1_ref[...])


def _outproj_ln(a1, a1s, a2, w, x, g, b, *, alpha):
    t_pad, d = x.shape
    tl = ROW_TILE
    n_prompt_tiles = t_pad // tl - 1
    k1 = a1.shape[1]
    k2 = a2.shape[1]
    has_decode_tile = a1s is not None
    last_a1 = a1.shape[0] // tl - 1
    in_specs = [pl.BlockSpec((tl, k1), lambda s: (jnp.minimum(s, last_a1), 0))]
    args = [a1]
    if has_decode_tile:
        in_specs.append(_resident((tl, k1), lambda s: (0, 0)))
        args.append(a1s)
    in_specs += [
        pl.BlockSpec((tl, k2), lambda s: (s, 0)),
        _resident((k1 + k2, d), lambda s: (0, 0)),
        pl.BlockSpec((tl, d), lambda s: (s, 0)),
        _resident((1, d), lambda s: (0, 0)),
        _resident((1, d), lambda s: (0, 0)),
    ]
    args += [a2, w, x, g.reshape(1, d), b.reshape(1, d)]
    return pl.pallas_call(
        functools.partial(_outproj_body, alpha=alpha, n_prompt_tiles=n_prompt_tiles,
                          has_decode_tile=has_decode_tile),
        grid=(n_prompt_tiles + 1,),
        in_specs=in_specs,
        out_specs=[pl.BlockSpec((tl, d), lambda s: (s, 0)), pl.BlockSpec((tl, d // 2), lambda s: (s, 0))],
        out_shape=[jax.ShapeDtypeStruct((t_pad, d), F32), jax.ShapeDtypeStruct((t_pad, d // 2), jnp.uint32)],
        compiler_params=_params(1),
        name="outproj_ln",
    )(*args)


def _router_body(x_ref, wt_ref, b_ref, e_ref, g_ref):
    logits = _dot_nt(wt_ref[...], x_ref[...], precision=lax.Precision.HIGHEST) + b_ref[...]
    n_e = logits.shape[0]
    eid = lax.broadcasted_iota(jnp.int32, logits.shape, 0).astype(F32)
    vals, idxs = [], []
    for _ in range(TOP_K):
        m = jnp.max(logits, axis=0, keepdims=True)
        idx = jnp.min(jnp.where(logits == m, eid, float(n_e)), axis=0, keepdims=True)
        vals.append(m)
        idxs.append(idx)
        logits = jnp.where(eid == idx, -jnp.inf, logits)
    ex = [jnp.exp(v - vals[0]) for v in vals]
    den = ex[0] + ex[1] + ex[2] + ex[3]
    e_ref[...] = jnp.concatenate(idxs, axis=0).astype(jnp.int32)
    g_ref[...] = jnp.concatenate([e / den for e in ex], axis=0)


def _router(x, w_router_t, b_router):
    t_pad, d = x.shape
    n_e = w_router_t.shape[0]
    tl = ROW_TILE
    return pl.pallas_call(
        _router_body,
        grid=(t_pad // tl,),
        in_specs=[
            pl.BlockSpec((tl, d), lambda i: (i, 0)),
            _resident((n_e, d), lambda i: (0, 0)),
            _resident((n_e, 1), lambda i: (0, 0)),
        ],
        out_specs=[pl.BlockSpec((TOP_K, tl), lambda i: (0, i))] * 2,
        out_shape=[jax.ShapeDtypeStruct((TOP_K, t_pad), jnp.int32), jax.ShapeDtypeStruct((TOP_K, t_pad), F32)],
        compiler_params=_params(1),
        name="router",
    )(x, w_router_t, b_router.reshape(n_e, 1))


def _route_plan(top_e, n_tok, n_e, n_items, t_pad):
    tk = n_tok * TOP_K
    e_flat = top_e[:, :n_tok].T.reshape(tk)
    onehot = (e_flat[:, None] == jnp.arange(n_e, dtype=jnp.int32)[None, :]).astype(jnp.int32)
    csum = jnp.cumsum(onehot, axis=0)
    rank = jnp.take_along_axis(csum, e_flat[:, None], axis=1)[:, 0] - 1
    counts = csum[-1]
    padded = (counts + SUBLANES - 1) // SUBLANES * SUBLANES
    seg_start = jnp.cumsum(padded) - padded
    dest = seg_start[e_flat] + rank
    n_rows = (tk + n_e * (SUBLANES - 1) + SUBLANES - 1) // SUBLANES * SUBLANES
    assert n_e * tk < 2 ** 31
    slot = jnp.arange(tk, dtype=jnp.int32)
    row_tok = (jnp.sort(e_flat * tk + slot) % tk) // TOP_K
    useg_start = jnp.cumsum(counts) - counts
    items_per_e = (counts + MOE_CAP - 1) // MOE_CAP
    item_end = jnp.cumsum(items_per_e)
    total = item_end[-1]
    q = jnp.arange(n_items, dtype=jnp.int32)
    valid = q < total
    e_q = jnp.clip(jnp.searchsorted(item_end, q, side="right"), 0, n_e - 1).astype(jnp.int32)
    e_q = jnp.where(valid, e_q, e_q[jnp.maximum(total - 1, 0)])
    within = q - (item_end - items_per_e)[e_q]
    item_start = jnp.where(valid, seg_start[e_q] + within * MOE_CAP, 0).astype(jnp.int32)
    item_ustart = jnp.where(valid, useg_start[e_q] + within * MOE_CAP, 0).astype(jnp.int32)
    item_start = jnp.concatenate([item_start, jnp.sum(padded, keepdims=True).astype(jnp.int32), item_ustart])
    item_n = jnp.where(valid, jnp.clip(counts[e_q] - within * MOE_CAP, 0, MOE_CAP), 0).astype(jnp.int32)
    pos = jnp.zeros((t_pad * TOP_K,), jnp.int32).at[:tk].set(dest.astype(jnp.int32))
    return e_q, item_start, item_n, row_tok, pos, n_rows


def _moe_body(ie_ref, ist_ref, in_ref, tok_ref,
              x_hbm, wg_ref, wu_ref, wd_ref, bg_ref, bu_ref, bd_ref, y_hbm,
              xstage, xb, acc, wgb, wub, wdb, zrows, sems, *, n_items, n_ff):
    q = pl.program_id(0)
    j = pl.program_id(1)
    n = in_ref[q]
    start = ist_ref[q]
    half = xstage.shape[1]
    tok_last = tok_ref.shape[0] - 1
    rows_per_body = MOE_CHUNK // n_ff

    def ceil8(v):
        return (v + SUBLANES - 1) // SUBLANES * SUBLANES

    def covered(item):
        return (in_ref[item] + MOE_CHUNK - 1) // MOE_CHUNK * MOE_CHUNK

    n8 = ceil8(n)
    nxt = jnp.minimum(q + 1, n_items - 1)
    nxt_base = ist_ref[n_items + 1 + nxt]
    nxt_rows = jnp.maximum(covered(q), ceil8(in_ref[nxt]))

    def gather_copy(r, tok):
        return pltpu.make_async_copy(x_hbm.at[pl.ds(tok, 1), :], xstage.at[pl.ds(r, 1), :], sems.at[0])

    def issue_rows(base, lo, hi):
        def body(gi, c):
            r0 = lo + gi * SUBLANES
            for u in range(SUBLANES):
                gather_copy(r0 + u, tok_ref[jnp.minimum(base + r0 + u, tok_last)]).start(priority=u % 2)
            return c

        lax.fori_loop(0, (hi - lo) // SUBLANES, body, 0)

    def wait_rows(count):
        def body(gi, c):
            for u in range(SUBLANES):
                gather_copy(gi * SUBLANES + u, 0).wait()
            return c

        lax.fori_loop(0, count // SUBLANES, body, 0)

    @pl.when((q == 0) & (j == 0))
    def _():
        xstage[...] = jnp.zeros(xstage.shape, jnp.uint32)
        zrows[...] = jnp.zeros(zrows.shape, F32)
        issue_rows(ist_ref[n_items + 1 + q], 0, n8)
        wait_rows(n8)

    @pl.when((q > 0) & (j == 0))
    def _():
        wait_rows(jnp.maximum(covered(jnp.maximum(q - 1, 0)), n8))

    @pl.when((j == 0) & (n > 0))
    def _():
        def cvt(c, carry):
            r0 = pl.multiple_of(c * MOE_CHUNK, MOE_CHUNK)
            w = xstage[pl.ds(r0, MOE_CHUNK), :]
            lo = pltpu.bitcast(lax.shift_left(w, jnp.uint32(16)), F32)
            hi = pltpu.bitcast(w & jnp.uint32(0xFFFF0000), F32)
            xb[pl.ds(r0, MOE_CHUNK), 0:half] = lo.astype(BF16)
            xb[pl.ds(r0, MOE_CHUNK), half:] = hi.astype(BF16)
            return carry

        lax.fori_loop(0, (n + MOE_CHUNK - 1) // MOE_CHUNK, cvt, 0)

    def out_copy(r0):
        return pltpu.make_async_copy(acc.at[pl.ds(pl.multiple_of(r0, SUBLANES), SUBLANES), :],
                                     y_hbm.at[pl.ds(pl.multiple_of(start + r0, SUBLANES), SUBLANES), :],
                                     sems.at[1])

    @pl.when(n > 0)
    def _():
        wgb[...] = wg_ref[...].astype(BF16)
        wub[...] = wu_ref[...].astype(BF16)
        wdb[...] = wd_ref[...].astype(BF16)

        chunks_per_step = covered(q) // MOE_CHUNK

        def chunk(r0, rows):
            slot0 = (j * chunks_per_step + lax.div(r0, jnp.int32(MOE_CHUNK))) * rows_per_body
            for u in range(rows_per_body):
                src = jnp.minimum(nxt_base + slot0 + u, tok_last)
                gather_copy(slot0 + u, tok_ref[src]).start(priority=u % 2)
            xc = xb[pl.ds(r0, rows), :]
            gate = jnp.dot(xc, wgb[...], preferred_element_type=F32) + bg_ref[...]
            lin = jnp.dot(xc, wub[...], preferred_element_type=F32) + bu_ref[...]
            glu = jnp.minimum(gate, SWIGLU_LIMIT)
            lin = jnp.clip(lin, -SWIGLU_LIMIT, SWIGLU_LIMIT)
            hid = glu * jax.nn.sigmoid(SWIGLU_ALPHA * glu) * (lin + 1.0)
            d = jnp.dot(hid.astype(BF16), wdb[...], preferred_element_type=F32)

            @pl.when(j == 0)
            def _():
                acc[pl.ds(r0, rows), :] = d + bd_ref[...]

            @pl.when(j > 0)
            def _():
                acc[pl.ds(r0, rows), :] += d

            @pl.when(j == n_ff - 1)
            def _():
                groups = jnp.minimum(n8 - r0, rows) // SUBLANES

                def put(gi, c2):
                    out_copy(r0 + gi * SUBLANES).start()
                    return c2

                lax.fori_loop(0, groups, put, 0)

        def full_chunk(c, carry):
            chunk(pl.multiple_of(c * MOE_CHUNK, MOE_CHUNK), MOE_CHUNK)
            return carry

        n_full = n // MOE_CHUNK
        rem = n - n_full * MOE_CHUNK
        lax.fori_loop(0, n_full, full_chunk, 0)
        r_rem = pl.multiple_of(n_full * MOE_CHUNK, MOE_CHUNK)

        @pl.when(rem > MOE_CHUNK // 2)
        def _():
            chunk(r_rem, MOE_CHUNK)

        @pl.when((rem > 0) & (rem <= MOE_CHUNK // 2))
        def _():
            chunk(r_rem, MOE_CHUNK // 2)

        @pl.when(j == n_ff - 1)
        def _():
            def wait_out(gi, c2):
                out_copy(gi * SUBLANES).wait()
                return c2

            lax.fori_loop(0, n8 // SUBLANES, wait_out, 0)

    @pl.when(j == n_ff - 1)
    def _():
        issue_rows(nxt_base, covered(q), nxt_rows)

    @pl.when((q == n_items - 1) & (j == n_ff - 1))
    def _():
        wait_rows(nxt_rows)
        used = ist_ref[n_items]
        n_tail = (y_hbm.shape[0] - used) // SUBLANES

        def tail_copy(gi):
            dst = pl.multiple_of(used + gi * SUBLANES, SUBLANES)
            return pltpu.make_async_copy(zrows, y_hbm.at[pl.ds(dst, SUBLANES), :], sems.at[1])

        def put(gi, c2):
            tail_copy(gi).start()
            return c2

        def wait(gi, c2):
            tail_copy(gi).wait()
            return c2

        lax.fori_loop(0, n_tail, put, 0)
        lax.fori_loop(0, n_tail, wait, 0)


def _moe_experts(xp, plan, w_gu, b_gu, w_down, b_down, layer, n_items):
    item_e, item_start, item_n, row_tok, _, n_rows = plan
    d = 2 * xp.shape[1]
    d_ff = w_down.shape[2]
    n_e = w_gu.shape[1]
    n_ff = d_ff // FF_TILE
    assert d_ff % FF_TILE == 0 and MOE_CAP % MOE_CHUNK == 0 and MOE_CHUNK % n_ff == 0

    def ff_tile(q, j, inn):
        return jnp.where(inn[q] > 0, j, n_ff - 1)

    bg4 = b_gu.reshape(b_gu.shape[0], n_e, 1, 2 * d_ff)
    bd4 = b_down.reshape(b_down.shape[0], n_e, 1, d)
    grid_spec = pltpu.PrefetchScalarGridSpec(
        num_scalar_prefetch=4,
        grid=(n_items, n_ff),
        in_specs=[
            pl.BlockSpec(memory_space=pl.ANY),
            pl.BlockSpec((None, None, d, FF_TILE), lambda q, j, ie, ist, inn, tok: (layer, ie[q], 0, ff_tile(q, j, inn))),
            pl.BlockSpec((None, None, d, FF_TILE),
                         lambda q, j, ie, ist, inn, tok: (layer, ie[q], 0, n_ff + ff_tile(q, j, inn))),
            pl.BlockSpec((None, None, FF_TILE, d), lambda q, j, ie, ist, inn, tok: (layer, ie[q], ff_tile(q, j, inn), 0)),
            pl.BlockSpec((None, None, 1, FF_TILE), lambda q, j, ie, ist, inn, tok: (layer, ie[q], 0, ff_tile(q, j, inn))),
            pl.BlockSpec((None, None, 1, FF_TILE),
                         lambda q, j, ie, ist, inn, tok: (layer, ie[q], 0, n_ff + ff_tile(q, j, inn))),
            pl.BlockSpec((None, None, 1, d), lambda q, j, ie, ist, inn, tok: (layer, ie[q], 0, 0)),
        ],
        out_specs=pl.BlockSpec(memory_space=pl.ANY),
        scratch_shapes=[
            pltpu.VMEM((MOE_CAP, d // 2), jnp.uint32),
            pltpu.VMEM((MOE_CAP, d), BF16),
            pltpu.VMEM((MOE_CAP, d), F32),
            pltpu.VMEM((d, FF_TILE), BF16),
            pltpu.VMEM((d, FF_TILE), BF16),
            pltpu.VMEM((FF_TILE, d), BF16),
            pltpu.VMEM((SUBLANES, d), F32),
            pltpu.SemaphoreType.DMA((2,)),
        ],
    )
    return pl.pallas_call(
        functools.partial(_moe_body, n_items=n_items, n_ff=n_ff),
        grid_spec=grid_spec,
        out_shape=jax.ShapeDtypeStruct((n_rows, d), F32),
        compiler_params=_params(2, MOE_VMEM_LIMIT),
        name="moe_experts",
    )(item_e, item_start, item_n, row_tok, xp, w_gu, w_gu, w_down, bg4, bg4, bd4)


def _combine_body(pos_ref, y_hbm, gate_ref, x_ref, g_ref, b_ref, o_ref, ybuf, sem, *, alpha, n_tok):
    s = pl.program_id(0)
    tm = x_ref.shape[0]
    n_valid = jnp.clip(n_tok - s * tm, 0, tm)

    def copy(r, k, src_row):
        return pltpu.make_async_copy(y_hbm.at[pl.ds(src_row, 1), :], ybuf.at[k, pl.ds(r, 1), :], sem.at[0])

    @pl.when(s == 0)
    def _():
        ybuf[...] = jnp.zeros(ybuf.shape, F32)

    def issue(r, c):
        base = (s * tm + r) * TOP_K
        for k in range(TOP_K):
            copy(r, k, pos_ref[base + k]).start(priority=k % 2)
        return c

    lax.fori_loop(0, n_valid, issue, 0)

    def wait(r, c):
        for k in range(TOP_K):
            copy(r, k, 0).wait()
        return c

    lax.fori_loop(0, n_valid, wait, 0)
    gates = gate_ref[...]
    y = gates[:, 0:1] * ybuf[0]
    for k in range(1, TOP_K):
        y = y + gates[:, k:k + 1] * ybuf[k]
    o_ref[...] = _layer_norm(alpha * x_ref[...] + y, g_ref[...], b_ref[...])


def _combine_ln(yr, pos, gates_tk, x, g, b, *, alpha, n_tok):
    t_pad, d = x.shape
    tm = COMBINE_TILE
    grid_spec = pltpu.PrefetchScalarGridSpec(
        num_scalar_prefetch=1,
        grid=(t_pad // tm,),
        in_specs=[
            pl.BlockSpec(memory_space=pl.ANY),
            pl.BlockSpec((tm, TOP_K), lambda s, p: (s, 0)),
            pl.BlockSpec((tm, d), lambda s, p: (s, 0)),
            _resident((1, d), lambda s, p: (0, 0)),
            _resident((1, d), lambda s, p: (0, 0)),
        ],
        out_specs=pl.BlockSpec((tm, d), lambda s, p: (s, 0)),
        scratch_shapes=[pltpu.VMEM((TOP_K, tm, d), F32), pltpu.SemaphoreType.DMA((1,))],
    )
    return pl.pallas_call(
        functools.partial(_combine_body, alpha=alpha, n_tok=n_tok),
        grid_spec=grid_spec,
        out_shape=jax.ShapeDtypeStruct((t_pad, d), F32),
        compiler_params=_params(1),
        name="combine_ln",
    )(pos, yr, gates_tk, x, g.reshape(1, d), b.reshape(1, d))


def _top_blocks_t(gate, row, n_valid_rows, n_rows):
    sel = jnp.zeros(gate.shape, F32)
    g = gate
    for _ in range(MOBA_TOPK):
        m = jnp.max(g, axis=0, keepdims=True)
        idx = jnp.min(jnp.where(g == m, row, float(n_rows)), axis=0, keepdims=True)
        pick = row == idx
        sel = jnp.where(pick, jnp.where(row < n_valid_rows, 1.0, sel), sel)
        g = jnp.where(pick, -jnp.inf, g)
    return sel


def _moba_body(q_ref, k_ref, v_ref, o_ref, kb, vt, kmean, sel_s, *, n_blocks):
    i = pl.program_id(2)
    blk = MOBA_BLOCK
    scale = HEAD_DIM ** -0.5

    @pl.when(i == 0)
    def _():
        kb[...] = k_ref[...].astype(BF16)
        for jb in range(n_blocks):
            rows = slice(jb * blk, (jb + 1) * blk)
            kmean[jb:jb + 1, :] = jnp.mean(k_ref[rows, :], axis=0, keepdims=True)
            vt[:, rows] = v_ref[rows, :].T.astype(BF16)

    qt = q_ref[...].T
    row = lax.broadcasted_iota(jnp.int32, (n_blocks, blk), 0).astype(F32)
    gate = jnp.dot(kmean[...], qt, preferred_element_type=F32, precision=lax.Precision.HIGHEST)
    own_f = i.astype(F32)
    gate = jnp.where(row < own_f, gate, NEG_INF)
    sel_s[...] = _top_blocks_t(gate, row, own_f, n_blocks)

    qtb = qt.astype(BF16)
    own = pl.multiple_of(i * blk, blk)
    st = jnp.dot(kb[pl.ds(own, blk), :], qtb, preferred_element_type=F32) * scale
    kk = lax.broadcasted_iota(jnp.int32, (blk, blk), 0)
    qq = lax.broadcasted_iota(jnp.int32, (blk, blk), 1)
    st = jnp.where(kk <= qq, st, NEG_INF)
    m0 = jnp.max(st, axis=0, keepdims=True)
    p = jnp.exp(st - m0)
    l0 = jnp.sum(p, axis=0, keepdims=True)
    a0 = jnp.dot(vt[:, pl.ds(own, blk)], p.astype(BF16), preferred_element_type=F32)

    def past(t, carry):
        m, l, a = carry
        starts, scores = [], []
        m_new = m
        for u in range(MOBA_GROUP):
            jb = jnp.minimum(t * MOBA_GROUP + u, n_blocks - 1)
            r0 = pl.multiple_of(jb * blk, blk)
            s_j = jnp.dot(kb[pl.ds(r0, blk), :], qtb, preferred_element_type=F32) * scale
            s_j = jnp.where(sel_s[pl.ds(jb, 1), :] > 0.0, s_j, NEG_INF)
            m_new = jnp.maximum(m_new, jnp.max(s_j, axis=0, keepdims=True))
            starts.append(r0)
            scores.append(s_j)
        corr = jnp.exp(m - m_new)
        l = corr * l
        a = corr * a
        for r0, s_j in zip(starts, scores):
            p_j = jnp.exp(s_j - m_new)
            l = l + jnp.sum(p_j, axis=0, keepdims=True)
            a = a + jnp.dot(vt[:, pl.ds(r0, blk)], p_j.astype(BF16), preferred_element_type=F32)
        return m_new, l, a

    n_trips = (i + MOBA_GROUP - 1) // MOBA_GROUP
    _, l_fin, a_fin = lax.fori_loop(0, n_trips, past, (m0, l0, a0))
    o_ref[...] = (a_fin / l_fin).T.astype(BF16)


def _moba_prompt(hq, kf, vf, *, n_seq, seq_len, n_heads):
    blk = MOBA_BLOCK
    n_blocks = seq_len // blk
    assert seq_len % blk == 0 and n_blocks >= MOBA_TOPK
    return pl.pallas_call(
        functools.partial(_moba_body, n_blocks=n_blocks),
        grid=(n_seq, n_heads, n_blocks),
        in_specs=[
            pl.BlockSpec((blk, HEAD_DIM), lambda b, h, i: (b * n_blocks + i, h)),
            pl.BlockSpec((seq_len, HEAD_DIM), lambda b, h, i: (b, h)),
            pl.BlockSpec((seq_len, HEAD_DIM), lambda b, h, i: (b, h)),
        ],
        out_specs=pl.BlockSpec((blk, HEAD_DIM), lambda b, h, i: (b * n_blocks + i, h)),
        out_shape=jax.ShapeDtypeStruct((n_seq * seq_len, n_heads * HEAD_DIM), BF16),
        scratch_shapes=[
            pltpu.VMEM((seq_len, HEAD_DIM), BF16),
            pltpu.VMEM((HEAD_DIM, seq_len), BF16),
            pltpu.VMEM((n_blocks, HEAD_DIM), F32),
            pltpu.VMEM((n_blocks, blk), F32),
        ],
        compiler_params=_params(3),
        name="moba_prompt",
    )(hq, kf, vf)


def _page_means_body(pt_ref, *refs, pages_per_block, blocks_per_step):
    o_ref = refs[-1]
    j = pl.program_id(1)
    n_heads = o_ref.shape[0]
    for blk in range(blocks_per_step):
        pages = refs[blk * pages_per_block:(blk + 1) * pages_per_block]
        for h in range(n_heads):
            tot = jnp.sum(pages[0][h], axis=0, keepdims=True)
            for r in pages[1:]:
                tot = tot + jnp.sum(r[h], axis=0, keepdims=True)
            o_ref[h, pl.ds(j * blocks_per_step + blk, 1), :] = tot * (1.0 / MOBA_BLOCK)


def _page_means(cache_k_hm, page_table_flat, *, db, n_pages):
    _, n_heads, page, hd = cache_k_hm.shape
    ppb = MOBA_BLOCK // page
    n_blocks = n_pages // ppb

    bps = PAGE_MEAN_BLOCKS if n_blocks % PAGE_MEAN_BLOCKS == 0 else 1
    pps = bps * ppb

    def page_spec(p):
        return pl.BlockSpec((None, n_heads, page, hd), lambda b, j, pt: (pt[b * n_pages + j * pps + p], 0, 0, 0))

    grid_spec = pltpu.PrefetchScalarGridSpec(
        num_scalar_prefetch=1,
        grid=(db, n_blocks // bps),
        in_specs=[page_spec(p) for p in range(pps)],
        out_specs=pl.BlockSpec((None, n_heads, n_blocks, hd), lambda b, j, pt: (b, 0, 0, 0)),
    )
    return pl.pallas_call(
        functools.partial(_page_means_body, pages_per_block=ppb, blocks_per_step=bps),
        grid_spec=grid_spec,
        out_shape=jax.ShapeDtypeStruct((db, n_heads, n_blocks, hd), F32),
        compiler_params=_params(2),
        name="page_means",
    )(page_table_flat, *([cache_k_hm] * pps))


def _select_body(q_ref, km_ref, o_ref, *, n_heads):
    b = pl.program_id(0)
    n_blocks = km_ref.shape[1]
    qrow = q_ref[pl.ds(b, 1), :]
    col = lax.broadcasted_iota(jnp.int32, (SUBLANES, n_blocks), 1).astype(F32)
    lane = lax.broadcasted_iota(jnp.int32, (1, LANES), 1)
    for h in range(n_heads):
        hs = slice(h * HEAD_DIM, (h + 1) * HEAD_DIM)
        q8 = jnp.broadcast_to(qrow[:, hs], (SUBLANES, HEAD_DIM))
        g = _dot_nt(q8, km_ref[h], precision=lax.Precision.HIGHEST)
        out = jnp.zeros((1, LANES), jnp.int32)
        for r in range(MOBA_TOPK):
            m = jnp.max(g, axis=1, keepdims=True)
            idx = jnp.min(jnp.where(g == m, col, float(n_blocks)), axis=1, keepdims=True)
            g = jnp.where(col == idx, -jnp.inf, g)
            out = jnp.where(lane == r, idx[0:1, :].astype(jnp.int32), out)
        o_ref[0, h:h + 1, :] = out


def _moba_select(hq, kmean_hs, *, row0, db, n_heads):
    n_blocks = kmean_hs.shape[2]
    w = n_heads * HEAD_DIM
    assert n_blocks >= MOBA_TOPK and row0 % db == 0
    return pl.pallas_call(
        functools.partial(_select_body, n_heads=n_heads),
        grid=(db,),
        in_specs=[
            pl.BlockSpec((db, w), lambda b: (row0 // db, 0)),
            pl.BlockSpec((None, n_heads, n_blocks, HEAD_DIM), lambda b: (b, 0, 0, 0)),
        ],
        out_specs=pl.BlockSpec((1, n_heads, LANES), lambda b: (b, 0, 0)),
        out_shape=jax.ShapeDtypeStruct((db, n_heads, LANES), jnp.int32),
        compiler_params=_params(1),
        name="moba_select",
    )(hq, kmean_hs)


def _decode_attn_body(pg_ref, q_ref, kn_ref, vn_ref, k_hbm, v_hbm, o_ref, kbuf, vbuf, sems,
                      *, n_heads, n_sel, n_steps):
    g = pl.program_id(0)
    page = kbuf.shape[2]
    scale = HEAD_DIM ** -0.5

    def copies(step, slot):
        h = step % n_heads
        out = []
        for s in range(n_sel):
            pg = pg_ref[step * n_sel + s]
            out.append(pltpu.make_async_copy(k_hbm.at[pg, h], kbuf.at[slot, s], sems.at[0, slot]))
            out.append(pltpu.make_async_copy(v_hbm.at[pg, h], vbuf.at[slot, s], sems.at[1, slot]))
        return out

    slot = g % 2

    @pl.when(g == 0)
    def _():
        for c in copies(0, 0):
            c.start()

    @pl.when(g + 1 < n_steps)
    def _():
        for c in copies(g + 1, 1 - slot):
            c.start()

    for c in copies(g, slot):
        c.wait()

    q = q_ref[...]
    q8 = jnp.broadcast_to(q, (SUBLANES, HEAD_DIM)).astype(BF16)
    k = kbuf[slot].reshape(n_sel * page, HEAD_DIM).astype(BF16)
    v = vbuf[slot].reshape(n_sel * page, HEAD_DIM).astype(BF16)
    sc = _dot_nt(q8, k) * scale
    s_own = jnp.sum(q * kn_ref[...], axis=-1, keepdims=True) * scale
    m = jnp.maximum(jnp.max(sc, axis=-1, keepdims=True), s_own)
    p = jnp.exp(sc - m)
    p_own = jnp.exp(s_own - m)
    den = jnp.sum(p, axis=-1, keepdims=True) + p_own
    acc = jnp.dot(p.astype(BF16), v, preferred_element_type=F32) + p_own * vn_ref[...]
    o_ref[...] = (acc / den)[0:1, :]


def _moba_decode(pages, q4, kn4, vn4, cache_k, cache_v, *, db, n_heads, n_sel):
    page = cache_k.shape[2]
    n_steps = db * n_heads
    one = pl.BlockSpec((None, None, 1, HEAD_DIM), lambda g, pg: (g // n_heads, g % n_heads, 0, 0))
    grid_spec = pltpu.PrefetchScalarGridSpec(
        num_scalar_prefetch=1,
        grid=(n_steps,),
        in_specs=[one, one, one, pl.BlockSpec(memory_space=pl.ANY), pl.BlockSpec(memory_space=pl.ANY)],
        out_specs=one,
        scratch_shapes=[pltpu.VMEM((2, n_sel, page, HEAD_DIM), F32), pltpu.VMEM((2, n_sel, page, HEAD_DIM), F32),
                        pltpu.SemaphoreType.DMA((2, 2))],
    )
    return pl.pallas_call(
        functools.partial(_decode_attn_body, n_heads=n_heads, n_sel=n_sel, n_steps=n_steps),
        grid_spec=grid_spec,
        out_shape=jax.ShapeDtypeStruct((db, n_heads, 1, HEAD_DIM), F32),
        compiler_params=_params(1),
        name="moba_decode",
    )(pages, q4, kn4, vn4, cache_k, cache_v)


def kernel(x_prompt, x_sample, cache_conv, cache_k, cache_v, cache_mem_k, cache_mem_v, page_table, mem_prompt,
           w_in_a, w_dw, b_dw, conv_ln_g, conv_ln_b, w_out_a, w_in_b, w_out_b, w_kv_shared, w_mem_kv,
           ln1_g, ln1_b, ln2_g, ln2_b, w_router, b_router, w_gu, b_gu, w_down, b_down):
    n_seq, seq_len, d = x_prompt.shape
    db, dseq, _ = x_sample.shape
    depth = ln1_g.shape[0]
    n_a = w_in_a.shape[0]
    n_e = w_router.shape[2]
    ch = w_dw.shape[2]
    moba_w = w_in_b.shape[2] - MEM_W
    n_heads = moba_w // HEAD_DIM
    n_mem = mem_prompt.shape[1]
    n_pool, page = cache_k.shape[0], cache_k.shape[1]
    n_pages = page_table.shape[1]
    past_len = n_pages * page
    alpha = (2 * depth) ** 0.25
    tl = ROW_TILE
    assert dseq == 1 and db % SUBLANES == 0 and db <= tl and seq_len % tl == 0
    assert ch + MEM_W == d and moba_w == ch and MOBA_BLOCK % page == 0 and past_len % MOBA_BLOCK == 0
    n_prompt = n_seq * seq_len
    n_tok = n_prompt + db
    t_pad = n_prompt + tl
    ppb = MOBA_BLOCK // page
    n_items = n_e + (n_tok * TOP_K) // MOE_CAP

    x = jnp.concatenate([x_prompt.reshape(n_prompt, d), x_sample.reshape(db, d),
                         jnp.zeros((tl - db, d), F32)], axis=0)

    pos = jnp.concatenate([jnp.tile(jnp.arange(seq_len, dtype=jnp.int32), n_seq),
                           jnp.full((db,), past_len, jnp.int32), jnp.zeros((tl - db,), jnp.int32)])
    half = HEAD_DIM // 2
    inv = ROPE_THETA ** (-jnp.arange(half, dtype=F32) / half)
    ang = pos.astype(F32)[:, None] * inv[None, :]
    rope = (jnp.concatenate([jnp.cos(ang), jnp.cos(ang)], axis=1),
            jnp.concatenate([-jnp.sin(ang), jnp.sin(ang)], axis=1))

    mem2 = mem_prompt.reshape(n_seq * n_mem, d)
    mkv = [_dense(mem2, w_mem_kv[l].astype(BF16), n_out=2, name="mem_kv") for l in range(depth)]
    new_mem_k = jnp.stack([kv[0] for kv in mkv]).reshape(depth, n_seq, n_mem, MEM_W)
    new_mem_v = jnp.stack([kv[1] for kv in mkv]).reshape(depth, n_seq, n_mem, MEM_W)
    mem_k_s = cache_mem_k.reshape(depth, db, n_mem, MEM_W)
    mem_v_s = cache_mem_v.reshape(depth, db, n_mem, MEM_W)

    pt_flat = page_table.reshape(-1).astype(jnp.int32)
    cache_k_hm = jnp.swapaxes(cache_k, 1, 2)
    cache_v_hm = jnp.swapaxes(cache_v, 1, 2)

    conv_tails, conv_new = [], []
    kf = vf = kmean_s = None
    for l in range(depth):
        if l < n_a:
            h = _dense(x, w_in_a[l].astype(BF16), name="in_proj_a")
            c_act, tail, us = _conv_module(h, jnp.swapaxes(cache_conv[l], 0, 1), w_dw[l], b_dw[l],
                                           conv_ln_g[l], conv_ln_b[l], n_seq=n_seq, seq_len=seq_len, ch=ch)
            conv_tails.append(tail[:, CONV_HALO - (CONV_W - 1):])
            conv_new.append(jnp.concatenate([cache_conv[l][:, 1:], us[:, None, :]], axis=1))
            m_act = _mem_attention(h, 2 * ch // MEM_W, new_mem_k[l], new_mem_v[l], mem_k_s[l], mem_v_s[l],
                                   n_seq=n_seq, seq_len=seq_len)
            a1, a1s, w_out = c_act, None, w_out_a[l]
        else:
            jb = l - n_a
            if kf is None:
                kf, vf = _dense(x, w_kv_shared.astype(BF16), rope=rope, n_rope_heads=n_heads, n_out=2,
                                name="kv_proj")
                kmean_s = _page_means(cache_k_hm, pt_flat, db=db, n_pages=n_pages)
            hq = _dense(x, w_in_b[jb].astype(BF16), rope=rope, n_rope_heads=n_heads, name="in_proj_b")
            o_prompt = _moba_prompt(hq, kf, vf, n_seq=n_seq, seq_len=seq_len, n_heads=n_heads)
            sel = _moba_select(hq, kmean_s, row0=n_prompt, db=db, n_heads=n_heads)[:, :, :MOBA_TOPK]
            blk_pages = sel[..., None] * ppb + jnp.arange(ppb, dtype=jnp.int32)
            pages = jnp.take_along_axis(page_table.astype(jnp.int32)[:, None, :],
                                        blk_pages.reshape(db, n_heads, MOBA_TOPK * ppb), axis=2)
            four = lambda a: a[n_prompt:n_tok].reshape(db, n_heads, 1, HEAD_DIM)
            o_s = _moba_decode(pages.reshape(-1), four(hq[:, :moba_w]), four(kf), four(vf), cache_k_hm, cache_v_hm,
                               db=db, n_heads=n_heads, n_sel=MOBA_TOPK * ppb)
            a1s = jnp.concatenate([o_s.reshape(db, moba_w), jnp.zeros((tl - db, moba_w), F32)], axis=0).astype(BF16)
            m_act = _mem_attention(hq, moba_w // MEM_W, new_mem_k[l], new_mem_v[l], mem_k_s[l], mem_v_s[l],
                                   n_seq=n_seq, seq_len=seq_len)
            a1, w_out = o_prompt, w_out_b[jb]
        x, xp = _outproj_ln(a1, a1s, m_act, w_out.astype(BF16), x, ln1_g[l], ln1_b[l], alpha=alpha)

        top_e, gates = _router(x, w_router[l].T, b_router[l])
        plan = _route_plan(top_e, n_tok, n_e, n_items, t_pad)
        yr = _moe_experts(xp, plan, w_gu, b_gu, w_down, b_down, l, n_items)
        x = _combine_ln(yr, plan[4], gates.T, x, ln2_g[l], ln2_b[l], alpha=alpha, n_tok=n_tok)

    y_prompt = x[:n_prompt].reshape(n_seq, seq_len, d)
    y_sample = x[n_prompt:n_tok].reshape(db, 1, d)
    heads4 = lambda a, n, s: a.reshape(n, s, n_heads, HEAD_DIM)
    mem5 = lambda a: a.reshape(depth, n_seq, n_mem, N_MEM_HEADS, HEAD_DIM)
    return (y_prompt, y_sample, jnp.stack(conv_tails), jnp.stack(conv_new),
            heads4(kf[:n_prompt], n_seq, seq_len), heads4(vf[:n_prompt], n_seq, seq_len),
            heads4(kf[n_prompt:n_tok], db, 1), heads4(vf[n_prompt:n_tok], db, 1),
            mem5(new_mem_k), mem5(new_mem_v))
```

```python
import functools

import jax
import jax.numpy as jnp
from jax import lax
from jax.experimental import pallas as pl
from jax.experimental.pallas import tpu as pltpu

F32 = jnp.float32
BF16 = jnp.bfloat16

HEAD_DIM = 128
N_MEM_HEADS = 4
MEM_W = N_MEM_HEADS * HEAD_DIM
CONV_W = 31
MOBA_BLOCK = 256
MOBA_TOPK = 3
ROPE_THETA = 10000.0
TOP_K = 4
SWIGLU_LIMIT = 7.0
SWIGLU_ALPHA = 1.702
LN_EPS = 1e-5
NEG_INF = -1e30

ROW_TILE = MOBA_BLOCK
CONV_HALO = 32
SUBLANES = 8
LANES = 128
FF_TILE = 512
MOE_CHUNK = 256
MOE_CAP = 1280
COMBINE_TILE = 128
MOBA_GROUP = 4
PAGE_MEAN_BLOCKS = 4
VMEM_LIMIT = 48 * 1024 * 1024
MOE_VMEM_LIMIT = 60 * 1024 * 1024


def _params(n_axes, vmem=VMEM_LIMIT):
    return pltpu.CompilerParams(dimension_semantics=("arbitrary",) * n_axes, vmem_limit_bytes=vmem)


def _resident(shape, index_map):
    return pl.BlockSpec(shape, index_map, pipeline_mode=pl.Buffered(1))


def _layer_norm(z, g, b):
    mu = jnp.mean(z, axis=-1, keepdims=True)
    zc = z - mu
    var = jnp.mean(zc * zc, axis=-1, keepdims=True)
    return zc * lax.rsqrt(var + LN_EPS) * g + b


def _dot_nt(a, b, **kw):
    return lax.dot_general(a, b, (((1,), (1,)), ((), ())), preferred_element_type=F32, **kw)


def _dense_body(x_ref, w_ref, *rest, n_rope_heads, col_chunk, n_out):
    o_refs = rest[-n_out:]
    if n_rope_heads:
        cos_ref, sin_ref = rest[0], rest[1]
    x = x_ref[...].astype(BF16)
    n = w_ref.shape[1]
    out_w = o_refs[0].shape[1]
    for c0 in range(0, n, col_chunk):
        cw = min(col_chunk, n - c0)
        y = jnp.dot(x, w_ref[:, c0:c0 + cw], preferred_element_type=F32)
        for h0 in range(0, cw, HEAD_DIM):
            blk = y[:, h0:h0 + HEAD_DIM]
            col = c0 + h0
            if col // HEAD_DIM < n_rope_heads:
                blk = blk * cos_ref[...] + pltpu.roll(blk, HEAD_DIM // 2, 1) * sin_ref[...]
            o_refs[col // out_w][:, col % out_w:col % out_w + HEAD_DIM] = blk


def _dense(x, w, rope=None, n_rope_heads=0, n_out=1, name="dense"):
    m, k = x.shape
    n = w.shape[1]
    tm = min(ROW_TILE, m)
    assert m % tm == 0 and n % (n_out * HEAD_DIM) == 0
    in_specs = [pl.BlockSpec((tm, k), lambda i: (i, 0)), _resident((k, n), lambda i: (0, 0))]
    args = [x, w]
    if n_rope_heads:
        in_specs += [pl.BlockSpec((tm, HEAD_DIM), lambda i: (i, 0))] * 2
        args += list(rope)
    outs = pl.pallas_call(
        functools.partial(_dense_body, n_rope_heads=n_rope_heads, col_chunk=512, n_out=n_out),
        grid=(m // tm,),
        in_specs=in_specs,
        out_specs=[pl.BlockSpec((tm, n // n_out), lambda i: (i, 0))] * n_out,
        out_shape=[jax.ShapeDtypeStruct((m, n // n_out), F32)] * n_out,
        compiler_params=_params(1),
        name=name,
    )(*args)
    return outs[0] if n_out == 1 else outs


def _conv_body(a_ref, g_ref, prev_ref, w_ref, b_ref, cg_ref, cb_ref, c_ref, tail_ref, us_ref, uext, ybuf,
               *, tiles_per_seq, n_prompt_tiles, db):
    s = pl.program_id(0)
    i = s % tiles_per_seq
    tl, ch = a_ref.shape
    u = a_ref[...] * jax.nn.sigmoid(g_ref[...])

    def ln_swish(y):
        z = _layer_norm(y, cg_ref[...], cb_ref[...])
        return z * jax.nn.sigmoid(z)

    @pl.when(s < n_prompt_tiles)
    def _():
        @pl.when(i == 0)
        def _():
            uext[0:CONV_HALO, :] = jnp.zeros((CONV_HALO, ch), F32)

        uext[CONV_HALO:CONV_HALO + tl, :] = u

        def lane_chunk(cc, carry):
            c0 = pl.multiple_of(cc * LANES, LANES)
            acc = jnp.zeros((tl, LANES), F32) + b_ref[:, pl.ds(c0, LANES)]
            shifted = [uext[p:p + tl + CONV_HALO - (SUBLANES if p else 0), pl.ds(c0, LANES)]
                       for p in range(SUBLANES)]
            for k in range(CONV_W):
                off = k + CONV_HALO - (CONV_W - 1)
                a, p = divmod(off, SUBLANES)
                acc = acc + w_ref[k:k + 1, pl.ds(c0, LANES)] * shifted[p][a * SUBLANES:a * SUBLANES + tl]
            ybuf[:, pl.ds(c0, LANES)] = acc
            return carry

        lax.fori_loop(0, ch // LANES, lane_chunk, 0)
        c_ref[...] = ln_swish(ybuf[...]).astype(BF16)

        @pl.when(i == tiles_per_seq - 1)
        def _():
            tail_ref[0] = uext[tl:tl + CONV_HALO, :]

        uext[0:CONV_HALO, :] = uext[tl:tl + CONV_HALO, :]

    @pl.when(s == n_prompt_tiles)
    def _():
        us = u[0:db, :]
        acc = b_ref[...] + w_ref[CONV_W - 1:CONV_W, :] * us
        for k in range(CONV_W - 1):
            acc = acc + w_ref[k:k + 1, :] * prev_ref[k]
        y = ln_swish(acc)
        c_ref[...] = jnp.concatenate([y, jnp.zeros((tl - db, ch), F32)], axis=0).astype(BF16)
        us_ref[...] = us


def _conv_module(h, prev_t, w_dw, b_dw, cg, cb, *, n_seq, seq_len, ch):
    t_pad = h.shape[0]
    tl = ROW_TILE
    tiles_per_seq = seq_len // tl
    n_prompt_tiles = n_seq * tiles_per_seq
    db = prev_t.shape[1]
    row = lambda v: v.reshape(1, ch)
    return pl.pallas_call(
        functools.partial(_conv_body, tiles_per_seq=tiles_per_seq, n_prompt_tiles=n_prompt_tiles, db=db),
        grid=(n_prompt_tiles + 1,),
        in_specs=[
            pl.BlockSpec((tl, ch), lambda s: (s, 0)),
            pl.BlockSpec((tl, ch), lambda s: (s, 1)),
            _resident((CONV_W - 1, db, ch), lambda s: (0, 0, 0)),
            _resident((CONV_W, ch), lambda s: (0, 0)),
            _resident((1, ch), lambda s: (0, 0)),
            _resident((1, ch), lambda s: (0, 0)),
            _resident((1, ch), lambda s: (0, 0)),
        ],
        out_specs=[
            pl.BlockSpec((tl, ch), lambda s: (s, 0)),
            pl.BlockSpec((1, CONV_HALO, ch), lambda s: (jnp.minimum(s // tiles_per_seq, n_seq - 1), 0, 0)),
            pl.BlockSpec((db, ch), lambda s: (0, 0)),
        ],
        out_shape=[
            jax.ShapeDtypeStruct((t_pad, ch), BF16),
            jax.ShapeDtypeStruct((n_seq, CONV_HALO, ch), F32),
            jax.ShapeDtypeStruct((db, ch), F32),
        ],
        scratch_shapes=[pltpu.VMEM((CONV_HALO + tl, ch), F32), pltpu.VMEM((tl, ch), F32)],
        compiler_params=_params(1),
        name="conv_module",
    )(h, h, prev_t, w_dw, row(b_dw), row(cg), row(cb))


def _attend_full(q, k, v):
    sc = _dot_nt(q, k) * (HEAD_DIM ** -0.5)
    p = jnp.exp(sc - jnp.max(sc, axis=-1, keepdims=True))
    o = jnp.dot(p.astype(BF16), v, preferred_element_type=F32)
    return o / jnp.sum(p, axis=-1, keepdims=True)


def _memattn_body(q_ref, mkp_ref, mvp_ref, mks_ref, mvs_ref, o_ref, *, n_prompt_tiles, db):
    s = pl.program_id(0)
    tl = q_ref.shape[0]
    heads = [slice(h * HEAD_DIM, (h + 1) * HEAD_DIM) for h in range(N_MEM_HEADS)]

    @pl.when(s < n_prompt_tiles)
    def _():
        q = q_ref[...].astype(BF16)
        k = mkp_ref[0].astype(BF16)
        v = mvp_ref[0].astype(BF16)
        for hs in heads:
            o_ref[:, hs] = _attend_full(q[:, hs], k[:, hs], v[:, hs]).astype(BF16)

    @pl.when(s == n_prompt_tiles)
    def _():
        q = q_ref[0:db, :].astype(BF16)
        row = lax.broadcasted_iota(jnp.int32, (db, HEAD_DIM), 0)
        outs = []
        for hs in heads:
            acc = jnp.zeros((db, HEAD_DIM), F32)
            for b in range(db):
                o = _attend_full(q[:, hs], mks_ref[b][:, hs].astype(BF16), mvs_ref[b][:, hs].astype(BF16))
                acc = jnp.where(row == b, o, acc)
            outs.append(acc)
        full = jnp.concatenate(outs, axis=1)
        o_ref[...] = jnp.concatenate([full, jnp.zeros((tl - db, MEM_W), F32)], axis=0).astype(BF16)


def _mem_attention(h, q_col_block, mkp, mvp, mks, mvs, *, n_seq, seq_len):
    t_pad = h.shape[0]
    tl = ROW_TILE
    tiles_per_seq = seq_len // tl
    n_prompt_tiles = n_seq * tiles_per_seq
    db, n_mem = mks.shape[0], mks.shape[1]
    seq_of = lambda s: (jnp.minimum(s // tiles_per_seq, n_seq - 1), 0, 0)
    return pl.pallas_call(
        functools.partial(_memattn_body, n_prompt_tiles=n_prompt_tiles, db=db),
        grid=(n_prompt_tiles + 1,),
        in_specs=[
            pl.BlockSpec((tl, MEM_W), lambda s: (s, q_col_block)),
            pl.BlockSpec((1, n_mem, MEM_W), seq_of),
            pl.BlockSpec((1, n_mem, MEM_W), seq_of),
            _resident((db, n_mem, MEM_W), lambda s: (0, 0, 0)),
            _resident((db, n_mem, MEM_W), lambda s: (0, 0, 0)),
        ],
        out_specs=pl.BlockSpec((tl, MEM_W), lambda s: (s, 0)),
        out_shape=jax.ShapeDtypeStruct((t_pad, MEM_W), BF16),
        compiler_params=_params(1),
        name="mem_attention",
    )(h, mkp, mvp, mks, mvs)


def _outproj_body(*refs, alpha, n_prompt_tiles, has_decode_tile):
    if has_decode_tile:
        a1_ref, a1s_ref, a2_ref, w_ref, x_ref, g_ref, b_ref, o_ref, op_ref = refs
    else:
        a1_ref, a2_ref, w_ref, x_ref, g_ref, b_ref, o_ref, op_ref = refs
    s = pl.program_id(0)
    k1 = a1_ref.shape[1]
    half = op_ref.shape[1]

    def finish(a1):
        y = jnp.dot(a1, w_ref[0:k1, :], preferred_element_type=F32)
        y = y + jnp.dot(a2_ref[...], w_ref[k1:, :], preferred_element_type=F32)
        xn = _layer_norm(alpha * x_ref[...] + y, g_ref[...], b_ref[...])
        o_ref[...] = xn
        bits = pltpu.bitcast(xn.astype(BF16).astype(F32), jnp.uint32)
        op_ref[...] = lax.shift_right_logical(bits[:, 0:half], jnp.uint32(16)) | (bits[:, half:] & jnp.uint32(0xFFFF0000))

    if has_decode_tile:
        @pl.when(s < n_prompt_tiles)
        def _():
            finish(a1_ref[...])

        @pl.when(s == n_prompt_tiles)
        def _():
            finish(a1s_ref[...])
    else:
        finish(a1_ref[...])


def _outproj_ln(a1, a1s, a2, w, x, g, b, *, alpha):
    t_pad, d = x.shape
    tl = ROW_TILE
    n_prompt_tiles = t_pad // tl - 1
    k1 = a1.shape[1]
    k2 = a2.shape[1]
    has_decode_tile = a1s is not None
    last_a1 = a1.shape[0] // tl - 1
    in_specs = [pl.BlockSpec((tl, k1), lambda s: (jnp.minimum(s, last_a1), 0))]
    args = [a1]
    if has_decode_tile:
        in_specs.append(_resident((tl, k1), lambda s: (0, 0)))
        args.append(a1s)
    in_specs += [
        pl.BlockSpec((tl, k2), lambda s: (s, 0)),
        _resident((k1 + k2, d), lambda s: (0, 0)),
        pl.BlockSpec((tl, d), lambda s: (s, 0)),
        _resident((1, d), lambda s: (0, 0)),
        _resident((1, d), lambda s: (0, 0)),
    ]
    args += [a2, w, x, g.reshape(1, d), b.reshape(1, d)]
    return pl.pallas_call(
        functools.partial(_outproj_body, alpha=alpha, n_prompt_tiles=n_prompt_tiles,
                          has_decode_tile=has_decode_tile),
        grid=(n_prompt_tiles + 1,),
        in_specs=in_specs,
        out_specs=[pl.BlockSpec((tl, d), lambda s: (s, 0)), pl.BlockSpec((tl, d // 2), lambda s: (s, 0))],
        out_shape=[jax.ShapeDtypeStruct((t_pad, d), F32), jax.ShapeDtypeStruct((t_pad, d // 2), jnp.uint32)],
        compiler_params=_params(1),
        name="outproj_ln",
    )(*args)


def _router_body(x_ref, wt_ref, b_ref, e_ref, g_ref):
    logits = _dot_nt(wt_ref[...], x_ref[...], precision=lax.Precision.HIGHEST) + b_ref[...]
    n_e = logits.shape[0]
    eid = lax.broadcasted_iota(jnp.int32, logits.shape, 0).astype(F32)
    vals, idxs = [], []
    for _ in range(TOP_K):
        m = jnp.max(logits, axis=0, keepdims=True)
        idx = jnp.min(jnp.where(logits == m, eid, float(n_e)), axis=0, keepdims=True)
        vals.append(m)
        idxs.append(idx)
        logits = jnp.where(eid == idx, -jnp.inf, logits)
    ex = [jnp.exp(v - vals[0]) for v in vals]
    den = ex[0] + ex[1] + ex[2] + ex[3]
    e_ref[...] = jnp.concatenate(idxs, axis=0).astype(jnp.int32)
    g_ref[...] = jnp.concatenate([e / den for e in ex], axis=0)


def _router(x, w_router_t, b_router):
    t_pad, d = x.shape
    n_e = w_router_t.shape[0]
    tl = ROW_TILE
    return pl.pallas_call(
        _router_body,
        grid=(t_pad // tl,),
        in_specs=[
            pl.BlockSpec((tl, d), lambda i: (i, 0)),
            _resident((n_e, d), lambda i: (0, 0)),
            _resident((n_e, 1), lambda i: (0, 0)),
        ],
        out_specs=[pl.BlockSpec((TOP_K, tl), lambda i: (0, i))] * 2,
        out_shape=[jax.ShapeDtypeStruct((TOP_K, t_pad), jnp.int32), jax.ShapeDtypeStruct((TOP_K, t_pad), F32)],
        compiler_params=_params(1),
        name="router",
    )(x, w_router_t, b_router.reshape(n_e, 1))


def _route_plan(top_e, n_tok, n_e, n_items, t_pad):
    tk = n_tok * TOP_K
    e_flat = top_e[:, :n_tok].T.reshape(tk)
    onehot = (e_flat[:, None] == jnp.arange(n_e, dtype=jnp.int32)[None, :]).astype(jnp.int32)
    csum = jnp.cumsum(onehot, axis=0)
    rank = jnp.take_along_axis(csum, e_flat[:, None], axis=1)[:, 0] - 1
    counts = csum[-1]
    padded = (counts + SUBLANES - 1) // SUBLANES * SUBLANES
    seg_start = jnp.cumsum(padded) - padded
    dest = seg_start[e_flat] + rank
    n_rows = (tk + n_e * (SUBLANES - 1) + SUBLANES - 1) // SUBLANES * SUBLANES
    assert n_e * tk < 2 ** 31
    slot = jnp.arange(tk, dtype=jnp.int32)
    row_tok = (jnp.sort(e_flat * tk + slot) % tk) // TOP_K
    useg_start = jnp.cumsum(counts) - counts
    items_per_e = (counts + MOE_CAP - 1) // MOE_CAP
    item_end = jnp.cumsum(items_per_e)
    total = item_end[-1]
    q = jnp.arange(n_items, dtype=jnp.int32)
    valid = q < total
    e_q = jnp.clip(jnp.searchsorted(item_end, q, side="right"), 0, n_e - 1).astype(jnp.int32)
    e_q = jnp.where(valid, e_q, e_q[jnp.maximum(total - 1, 0)])
    within = q - (item_end - items_per_e)[e_q]
    item_start = jnp.where(valid, seg_start[e_q] + within * MOE_CAP, 0).astype(jnp.int32)
    item_ustart = jnp.where(valid, useg_start[e_q] + within * MOE_CAP, 0).astype(jnp.int32)
    item_start = jnp.concatenate([item_start, jnp.sum(padded, keepdims=True).astype(jnp.int32), item_ustart])
    item_n = jnp.where(valid, jnp.clip(counts[e_q] - within * MOE_CAP, 0, MOE_CAP), 0).astype(jnp.int32)
    pos = jnp.zeros((t_pad * TOP_K,), jnp.int32).at[:tk].set(dest.astype(jnp.int32))
    return e_q, item_start, item_n, row_tok, pos, n_rows


def _moe_body(ie_ref, ist_ref, in_ref, tok_ref,
              x_hbm, wg_ref, wu_ref, wd_ref, bg_ref, bu_ref, bd_ref, y_hbm,
              xstage, xb, acc, wgb, wub, wdb, zrows, sems, *, n_items, n_ff):
    q = pl.program_id(0)
    j = pl.program_id(1)
    n = in_ref[q]
    start = ist_ref[q]
    half = xstage.shape[1]
    tok_last = tok_ref.shape[0] - 1
    rows_per_body = MOE_CHUNK // n_ff

    def ceil8(v):
        return (v + SUBLANES - 1) // SUBLANES * SUBLANES

    def covered(item):
        return (in_ref[item] + MOE_CHUNK - 1) // MOE_CHUNK * MOE_CHUNK

    n8 = ceil8(n)
    nxt = jnp.minimum(q + 1, n_items - 1)
    nxt_base = ist_ref[n_items + 1 + nxt]
    nxt_rows = jnp.maximum(covered(q), ceil8(in_ref[nxt]))

    def gather_copy(r, tok):
        return pltpu.make_async_copy(x_hbm.at[pl.ds(tok, 1), :], xstage.at[pl.ds(r, 1), :], sems.at[0])

    def issue_rows(base, lo, hi):
        def body(gi, c):
            r0 = lo + gi * SUBLANES
            for u in range(SUBLANES):
                gather_copy(r0 + u, tok_ref[jnp.minimum(base + r0 + u, tok_last)]).start(priority=u % 2)
            return c

        lax.fori_loop(0, (hi - lo) // SUBLANES, body, 0)

    def wait_rows(count):
        def body(gi, c):
            for u in range(SUBLANES):
                gather_copy(gi * SUBLANES + u, 0).wait()
            return c

        lax.fori_loop(0, count // SUBLANES, body, 0)

    @pl.when((q == 0) & (j == 0))
    def _():
        xstage[...] = jnp.zeros(xstage.shape, jnp.uint32)
        zrows[...] = jnp.zeros(zrows.shape, F32)
        issue_rows(ist_ref[n_items + 1 + q], 0, n8)
        wait_rows(n8)

    @pl.when((q > 0) & (j == 0))
    def _():
        wait_rows(jnp.maximum(covered(jnp.maximum(q - 1, 0)), n8))

    @pl.when((j == 0) & (n > 0))
    def _():
        def cvt(c, carry):
            r0 = pl.multiple_of(c * MOE_CHUNK, MOE_CHUNK)
            w = xstage[pl.ds(r0, MOE_CHUNK), :]
            lo = pltpu.bitcast(lax.shift_left(w, jnp.uint32(16)), F32)
            hi = pltpu.bitcast(w & jnp.uint32(0xFFFF0000), F32)
            xb[pl.ds(r0, MOE_CHUNK), 0:half] = lo.astype(BF16)
            xb[pl.ds(r0, MOE_CHUNK), half:] = hi.astype(BF16)
            return carry

        lax.fori_loop(0, (n + MOE_CHUNK - 1) // MOE_CHUNK, cvt, 0)

    def out_copy(r0):
        return pltpu.make_async_copy(acc.at[pl.ds(pl.multiple_of(r0, SUBLANES), SUBLANES), :],
                                     y_hbm.at[pl.ds(pl.multiple_of(start + r0, SUBLANES), SUBLANES), :],
                                     sems.at[1])

    @pl.when(n > 0)
    def _():
        wgb[...] = wg_ref[...].astype(BF16)
        wub[...] = wu_ref[...].astype(BF16)
        wdb[...] = wd_ref[...].astype(BF16)

        chunks_per_step = covered(q) // MOE_CHUNK

        def chunk(r0, rows):
            slot0 = (j * chunks_per_step + lax.div(r0, jnp.int32(MOE_CHUNK))) * rows_per_body
            for u in range(rows_per_body):
                src = jnp.minimum(nxt_base + slot0 + u, tok_last)
                gather_copy(slot0 + u, tok_ref[src]).start(priority=u % 2)
            xc = xb[pl.ds(r0, rows), :]
            gate = jnp.dot(xc, wgb[...], preferred_element_type=F32) + bg_ref[...]
            lin = jnp.dot(xc, wub[...], preferred_element_type=F32) + bu_ref[...]
            glu = jnp.minimum(gate, SWIGLU_LIMIT)
            lin = jnp.clip(lin, -SWIGLU_LIMIT, SWIGLU_LIMIT)
            hid = glu * jax.nn.sigmoid(SWIGLU_ALPHA * glu) * (lin + 1.0)
            d = jnp.dot(hid.astype(BF16), wdb[...], preferred_element_type=F32)

            @pl.when(j == 0)
            def _():
                acc[pl.ds(r0, rows), :] = d + bd_ref[...]

            @pl.when(j > 0)
            def _():
                acc[pl.ds(r0, rows), :] += d

            @pl.when(j == n_ff - 1)
            def _():
                groups = jnp.minimum(n8 - r0, rows) // SUBLANES

                def put(gi, c2):
                    out_copy(r0 + gi * SUBLANES).start()
                    return c2

                lax.fori_loop(0, groups, put, 0)

        def full_chunk(c, carry):
            chunk(pl.multiple_of(c * MOE_CHUNK, MOE_CHUNK), MOE_CHUNK)
            return carry

        n_full = n // MOE_CHUNK
        rem = n - n_full * MOE_CHUNK
        lax.fori_loop(0, n_full, full_chunk, 0)
        r_rem = pl.multiple_of(n_full * MOE_CHUNK, MOE_CHUNK)

        @pl.when(rem > MOE_CHUNK // 2)
        def _():
            chunk(r_rem, MOE_CHUNK)

        @pl.when((rem > 0) & (rem <= MOE_CHUNK // 2))
        def _():
            chunk(r_rem, MOE_CHUNK // 2)

        @pl.when(j == n_ff - 1)
        def _():
            def wait_out(gi, c2):
                out_copy(gi * SUBLANES).wait()
                return c2

            lax.fori_loop(0, n8 // SUBLANES, wait_out, 0)

    @pl.when(j == n_ff - 1)
    def _():
        issue_rows(nxt_base, covered(q), nxt_rows)

    @pl.when((q == n_items - 1) & (j == n_ff - 1))
    def _():
        wait_rows(nxt_rows)
        used = ist_ref[n_items]
        n_tail = (y_hbm.shape[0] - used) // SUBLANES

        def tail_copy(gi):
            dst = pl.multiple_of(used + gi * SUBLANES, SUBLANES)
            return pltpu.make_async_copy(zrows, y_hbm.at[pl.ds(dst, SUBLANES), :], sems.at[1])

        def put(gi, c2):
            tail_copy(gi).start()
            return c2

        def wait(gi, c2):
            tail_copy(gi).wait()
            return c2

        lax.fori_loop(0, n_tail, put, 0)
        lax.fori_loop(0, n_tail, wait, 0)


def _moe_experts(xp, plan, w_gu, b_gu, w_down, b_down, layer, n_items):
    item_e, item_start, item_n, row_tok, _, n_rows = plan
    d = 2 * xp.shape[1]
    d_ff = w_down.shape[2]
    n_e = w_gu.shape[1]
    n_ff = d_ff // FF_TILE
    assert d_ff % FF_TILE == 0 and MOE_CAP % MOE_CHUNK == 0 and MOE_CHUNK % n_ff == 0

    def ff_tile(q, j, inn):
        return jnp.where(inn[q] > 0, j, n_ff - 1)

    bg4 = b_gu.reshape(b_gu.shape[0], n_e, 1, 2 * d_ff)
    bd4 = b_down.reshape(b_down.shape[0], n_e, 1, d)
    grid_spec = pltpu.PrefetchScalarGridSpec(
        num_scalar_prefetch=4,
        grid=(n_items, n_ff),
        in_specs=[
            pl.BlockSpec(memory_space=pl.ANY),
            pl.BlockSpec((None, None, d, FF_TILE), lambda q, j, ie, ist, inn, tok: (layer, ie[q], 0, ff_tile(q, j, inn))),
            pl.BlockSpec((None, None, d, FF_TILE),
                         lambda q, j, ie, ist, inn, tok: (layer, ie[q], 0, n_ff + ff_tile(q, j, inn))),
            pl.BlockSpec((None, None, FF_TILE, d), lambda q, j, ie, ist, inn, tok: (layer, ie[q], ff_tile(q, j, inn), 0)),
            pl.BlockSpec((None, None, 1, FF_TILE), lambda q, j, ie, ist, inn, tok: (layer, ie[q], 0, ff_tile(q, j, inn))),
            pl.BlockSpec((None, None, 1, FF_TILE),
                         lambda q, j, ie, ist, inn, tok: (layer, ie[q], 0, n_ff + ff_tile(q, j, inn))),
            pl.BlockSpec((None, None, 1, d), lambda q, j, ie, ist, inn, tok: (layer, ie[q], 0, 0)),
        ],
        out_specs=pl.BlockSpec(memory_space=pl.ANY),
        scratch_shapes=[
            pltpu.VMEM((MOE_CAP, d // 2), jnp.uint32),
            pltpu.VMEM((MOE_CAP, d), BF16),
            pltpu.VMEM((MOE_CAP, d), F32),
            pltpu.VMEM((d, FF_TILE), BF16),
            pltpu.VMEM((d, FF_TILE), BF16),
            pltpu.VMEM((FF_TILE, d), BF16),
            pltpu.VMEM((SUBLANES, d), F32),
            pltpu.SemaphoreType.DMA((2,)),
        ],
    )
    return pl.pallas_call(
        functools.partial(_moe_body, n_items=n_items, n_ff=n_ff),
        grid_spec=grid_spec,
        out_shape=jax.ShapeDtypeStruct((n_rows, d), F32),
        compiler_params=_params(2, MOE_VMEM_LIMIT),
        name="moe_experts",
    )(item_e, item_start, item_n, row_tok, xp, w_gu, w_gu, w_down, bg4, bg4, bd4)


def _combine_body(pos_ref, y_hbm, gate_ref, x_ref, g_ref, b_ref, o_ref, ybuf, sem, *, alpha):
    s = pl.program_id(0)
    n_steps = pl.num_programs(0)
    tm = x_ref.shape[0]
    cur = s % 2

    def copy(slot, r, k, src_row):
        return pltpu.make_async_copy(y_hbm.at[pl.ds(src_row, 1), :], ybuf.at[slot, k, pl.ds(r, 1), :],
                                     sem.at[slot])

    def issue_tile(tile, slot):
        for r in range(tm):
            base = (tile * tm + r) * TOP_K
            for k in range(TOP_K):
                copy(slot, r, k, pos_ref[base + k]).start(priority=k % 2)

    def wait_tile(slot):
        for r in range(tm):
            for k in range(TOP_K):
                copy(slot, r, k, 0).wait()

    @pl.when(s == 0)
    def _():
        issue_tile(0, 0)

    issue_tile(jnp.minimum(s + 1, n_steps - 1), 1 - cur)
    wait_tile(cur)
    gates = gate_ref[...]
    y = gates[:, 0:1] * ybuf[cur, 0]
    for k in range(1, TOP_K):
        y = y + gates[:, k:k + 1] * ybuf[cur, k]
    o_ref[...] = _layer_norm(alpha * x_ref[...] + y, g_ref[...], b_ref[...])

    @pl.when(s == n_steps - 1)
    def _():
        wait_tile(1 - cur)


def _combine_ln(yr, pos, gates_tk, x, g, b, *, alpha):
    t_pad, d = x.shape
    tm = COMBINE_TILE
    grid_spec = pltpu.PrefetchScalarGridSpec(
        num_scalar_prefetch=1,
        grid=(t_pad // tm,),
        in_specs=[
            pl.BlockSpec(memory_space=pl.ANY),
            pl.BlockSpec((tm, TOP_K), lambda s, p: (s, 0)),
            pl.BlockSpec((tm, d), lambda s, p: (s, 0)),
            _resident((1, d), lambda s, p: (0, 0)),
            _resident((1, d), lambda s, p: (0, 0)),
        ],
        out_specs=pl.BlockSpec((tm, d), lambda s, p: (s, 0)),
        scratch_shapes=[pltpu.VMEM((2, TOP_K, tm, d), F32), pltpu.SemaphoreType.DMA((2,))],
    )
    return pl.pallas_call(
        functools.partial(_combine_body, alpha=alpha),
        grid_spec=grid_spec,
        out_shape=jax.ShapeDtypeStruct((t_pad, d), F32),
        compiler_params=_params(1),
        name="combine_ln",
    )(pos, yr, gates_tk, x, g.reshape(1, d), b.reshape(1, d))


def _top_blocks_t(gate, row, n_valid_rows, n_rows):
    sel = jnp.zeros(gate.shape, F32)
    g = gate
    for _ in range(MOBA_TOPK):
        m = jnp.max(g, axis=0, keepdims=True)
        idx = jnp.min(jnp.where(g == m, row, float(n_rows)), axis=0, keepdims=True)
        pick = row == idx
        sel = jnp.where(pick, jnp.where(row < n_valid_rows, 1.0, sel), sel)
        g = jnp.where(pick, -jnp.inf, g)
    return sel


def _moba_body(q_ref, k_ref, v_ref, o_ref, kb, vt, kmean, sel_s, *, n_blocks):
    i = pl.program_id(2)
    blk = MOBA_BLOCK
    scale = HEAD_DIM ** -0.5

    @pl.when(i == 0)
    def _():
        kb[...] = k_ref[...].astype(BF16)
        for jb in range(n_blocks):
            rows = slice(jb * blk, (jb + 1) * blk)
            kmean[jb:jb + 1, :] = jnp.mean(k_ref[rows, :], axis=0, keepdims=True)
            vt[:, rows] = v_ref[rows, :].T.astype(BF16)

    qt = q_ref[...].T
    row = lax.broadcasted_iota(jnp.int32, (n_blocks, blk), 0).astype(F32)
    gate = jnp.dot(kmean[...], qt, preferred_element_type=F32, precision=lax.Precision.HIGHEST)
    own_f = i.astype(F32)
    gate = jnp.where(row < own_f, gate, NEG_INF)
    sel_s[...] = _top_blocks_t(gate, row, own_f, n_blocks)

    qtb = qt.astype(BF16)
    own = pl.multiple_of(i * blk, blk)
    st = jnp.dot(kb[pl.ds(own, blk), :], qtb, preferred_element_type=F32) * scale
    kk = lax.broadcasted_iota(jnp.int32, (blk, blk), 0)
    qq = lax.broadcasted_iota(jnp.int32, (blk, blk), 1)
    st = jnp.where(kk <= qq, st, NEG_INF)
    m0 = jnp.max(st, axis=0, keepdims=True)
    p = jnp.exp(st - m0)
    l0 = jnp.sum(p, axis=0, keepdims=True)
    a0 = jnp.dot(vt[:, pl.ds(own, blk)], p.astype(BF16), preferred_element_type=F32)

    def past(t, carry):
        m, l, a = carry
        starts, scores = [], []
        m_new = m
        for u in range(MOBA_GROUP):
            jb = jnp.minimum(t * MOBA_GROUP + u, n_blocks - 1)
            r0 = pl.multiple_of(jb * blk, blk)
            s_j = jnp.dot(kb[pl.ds(r0, blk), :], qtb, preferred_element_type=F32) * scale
            s_j = jnp.where(sel_s[pl.ds(jb, 1), :] > 0.0, s_j, NEG_INF)
            m_new = jnp.maximum(m_new, jnp.max(s_j, axis=0, keepdims=True))
            starts.append(r0)
            scores.append(s_j)
        corr = jnp.exp(m - m_new)
        l = corr * l
        a = corr * a
        for r0, s_j in zip(starts, scores):
            p_j = jnp.exp(s_j - m_new)
            l = l + jnp.sum(p_j, axis=0, keepdims=True)
            a = a + jnp.dot(vt[:, pl.ds(r0, blk)], p_j.astype(BF16), preferred_element_type=F32)
        return m_new, l, a

    n_trips = (i + MOBA_GROUP - 1) // MOBA_GROUP
    _, l_fin, a_fin = lax.fori_loop(0, n_trips, past, (m0, l0, a0))
    o_ref[...] = (a_fin / l_fin).T.astype(BF16)


def _moba_prompt(hq, kf, vf, *, n_seq, seq_len, n_heads):
    blk = MOBA_BLOCK
    n_blocks = seq_len // blk
    assert seq_len % blk == 0 and n_blocks >= MOBA_TOPK
    return pl.pallas_call(
        functools.partial(_moba_body, n_blocks=n_blocks),
        grid=(n_seq, n_heads, n_blocks),
        in_specs=[
            pl.BlockSpec((blk, HEAD_DIM), lambda b, h, i: (b * n_blocks + i, h)),
            pl.BlockSpec((seq_len, HEAD_DIM), lambda b, h, i: (b, h)),
            pl.BlockSpec((seq_len, HEAD_DIM), lambda b, h, i: (b, h)),
        ],
        out_specs=pl.BlockSpec((blk, HEAD_DIM), lambda b, h, i: (b * n_blocks + i, h)),
        out_shape=jax.ShapeDtypeStruct((n_seq * seq_len, n_heads * HEAD_DIM), BF16),
        scratch_shapes=[
            pltpu.VMEM((seq_len, HEAD_DIM), BF16),
            pltpu.VMEM((HEAD_DIM, seq_len), BF16),
            pltpu.VMEM((n_blocks, HEAD_DIM), F32),
            pltpu.VMEM((n_blocks, blk), F32),
        ],
        compiler_params=_params(3),
        name="moba_prompt",
    )(hq, kf, vf)


def _page_means_body(pt_ref, *refs, pages_per_block, blocks_per_step):
    o_ref = refs[-1]
    j = pl.program_id(1)
    n_heads = o_ref.shape[0]
    for blk in range(blocks_per_step):
        pages = refs[blk * pages_per_block:(blk + 1) * pages_per_block]
        for h in range(n_heads):
            tot = jnp.sum(pages[0][h], axis=0, keepdims=True)
            for r in pages[1:]:
                tot = tot + jnp.sum(r[h], axis=0, keepdims=True)
            o_ref[h, pl.ds(j * blocks_per_step + blk, 1), :] = tot * (1.0 / MOBA_BLOCK)


def _page_means(cache_k_hm, page_table_flat, *, db, n_pages):
    _, n_heads, page, hd = cache_k_hm.shape
    ppb = MOBA_BLOCK // page
    n_blocks = n_pages // ppb

    bps = PAGE_MEAN_BLOCKS if n_blocks % PAGE_MEAN_BLOCKS == 0 else 1
    pps = bps * ppb

    def page_spec(p):
        return pl.BlockSpec((None, n_heads, page, hd), lambda b, j, pt: (pt[b * n_pages + j * pps + p], 0, 0, 0))

    grid_spec = pltpu.PrefetchScalarGridSpec(
        num_scalar_prefetch=1,
        grid=(db, n_blocks // bps),
        in_specs=[page_spec(p) for p in range(pps)],
        out_specs=pl.BlockSpec((None, n_heads, n_blocks, hd), lambda b, j, pt: (b, 0, 0, 0)),
    )
    return pl.pallas_call(
        functools.partial(_page_means_body, pages_per_block=ppb, blocks_per_step=bps),
        grid_spec=grid_spec,
        out_shape=jax.ShapeDtypeStruct((db, n_heads, n_blocks, hd), F32),
        compiler_params=_params(2),
        name="page_means",
    )(page_table_flat, *([cache_k_hm] * pps))


def _select_body(q_ref, km_ref, o_ref, *, n_heads):
    b = pl.program_id(0)
    n_blocks = km_ref.shape[1]
    qrow = q_ref[pl.ds(b, 1), :]
    col = lax.broadcasted_iota(jnp.int32, (SUBLANES, n_blocks), 1).astype(F32)
    lane = lax.broadcasted_iota(jnp.int32, (1, LANES), 1)
    for h in range(n_heads):
        hs = slice(h * HEAD_DIM, (h + 1) * HEAD_DIM)
        q8 = jnp.broadcast_to(qrow[:, hs], (SUBLANES, HEAD_DIM))
        g = _dot_nt(q8, km_ref[h], precision=lax.Precision.HIGHEST)
        out = jnp.zeros((1, LANES), jnp.int32)
        for r in range(MOBA_TOPK):
            m = jnp.max(g, axis=1, keepdims=True)
            idx = jnp.min(jnp.where(g == m, col, float(n_blocks)), axis=1, keepdims=True)
            g = jnp.where(col == idx, -jnp.inf, g)
            out = jnp.where(lane == r, idx[0:1, :].astype(jnp.int32), out)
        o_ref[0, h:h + 1, :] = out


def _moba_select(hq, kmean_hs, *, row0, db, n_heads):
    n_blocks = kmean_hs.shape[2]
    w = n_heads * HEAD_DIM
    assert n_blocks >= MOBA_TOPK and row0 % db == 0
    return pl.pallas_call(
        functools.partial(_select_body, n_heads=n_heads),
        grid=(db,),
        in_specs=[
            pl.BlockSpec((db, w), lambda b: (row0 // db, 0)),
            pl.BlockSpec((None, n_heads, n_blocks, HEAD_DIM), lambda b: (b, 0, 0, 0)),
        ],
        out_specs=pl.BlockSpec((1, n_heads, LANES), lambda b: (b, 0, 0)),
        out_shape=jax.ShapeDtypeStruct((db, n_heads, LANES), jnp.int32),
        compiler_params=_params(1),
        name="moba_select",
    )(hq, kmean_hs)


def _decode_attn_body(pg_ref, q_ref, kn_ref, vn_ref, k_hbm, v_hbm, o_ref, kbuf, vbuf, sems,
                      *, n_heads, n_sel, n_steps):
    g = pl.program_id(0)
    page = kbuf.shape[2]
    scale = HEAD_DIM ** -0.5

    def copies(step, slot):
        h = step % n_heads
        out = []
        for s in range(n_sel):
            pg = pg_ref[step * n_sel + s]
            out.append(pltpu.make_async_copy(k_hbm.at[pg, h], kbuf.at[slot, s], sems.at[0, slot]))
            out.append(pltpu.make_async_copy(v_hbm.at[pg, h], vbuf.at[slot, s], sems.at[1, slot]))
        return out

    slot = g % 2

    @pl.when(g == 0)
    def _():
        for c in copies(0, 0):
            c.start()

    @pl.when(g + 1 < n_steps)
    def _():
        for c in copies(g + 1, 1 - slot):
            c.start()

    for c in copies(g, slot):
        c.wait()

    q = q_ref[...]
    q8 = jnp.broadcast_to(q, (SUBLANES, HEAD_DIM)).astype(BF16)
    k = kbuf[slot].reshape(n_sel * page, HEAD_DIM).astype(BF16)
    v = vbuf[slot].reshape(n_sel * page, HEAD_DIM).astype(BF16)
    sc = _dot_nt(q8, k) * scale
    s_own = jnp.sum(q * kn_ref[...], axis=-1, keepdims=True) * scale
    m = jnp.maximum(jnp.max(sc, axis=-1, keepdims=True), s_own)
    p = jnp.exp(sc - m)
    p_own = jnp.exp(s_own - m)
    den = jnp.sum(p, axis=-1, keepdims=True) + p_own
    acc = jnp.dot(p.astype(BF16), v, preferred_element_type=F32) + p_own * vn_ref[...]
    o_ref[...] = (acc / den)[0:1, :]


def _moba_decode(pages, q4, kn4, vn4, cache_k, cache_v, *, db, n_heads, n_sel):
    page = cache_k.shape[2]
    n_steps = db * n_heads
    one = pl.BlockSpec((None, None, 1, HEAD_DIM), lambda g, pg: (g // n_heads, g % n_heads, 0, 0))
    grid_spec = pltpu.PrefetchScalarGridSpec(
        num_scalar_prefetch=1,
        grid=(n_steps,),
        in_specs=[one, one, one, pl.BlockSpec(memory_space=pl.ANY), pl.BlockSpec(memory_space=pl.ANY)],
        out_specs=one,
        scratch_shapes=[pltpu.VMEM((2, n_sel, page, HEAD_DIM), F32), pltpu.VMEM((2, n_sel, page, HEAD_DIM), F32),
                        pltpu.SemaphoreType.DMA((2, 2))],
    )
    return pl.pallas_call(
        functools.partial(_decode_attn_body, n_heads=n_heads, n_sel=n_sel, n_steps=n_steps),
        grid_spec=grid_spec,
        out_shape=jax.ShapeDtypeStruct((db, n_heads, 1, HEAD_DIM), F32),
        compiler_params=_params(1),
        name="moba_decode",
    )(pages, q4, kn4, vn4, cache_k, cache_v)


def kernel(x_prompt, x_sample, cache_conv, cache_k, cache_v, cache_mem_k, cache_mem_v, page_table, mem_prompt,
           w_in_a, w_dw, b_dw, conv_ln_g, conv_ln_b, w_out_a, w_in_b, w_out_b, w_kv_shared, w_mem_kv,
           ln1_g, ln1_b, ln2_g, ln2_b, w_router, b_router, w_gu, b_gu, w_down, b_down):
    n_seq, seq_len, d = x_prompt.shape
    db, dseq, _ = x_sample.shape
    depth = ln1_g.shape[0]
    n_a = w_in_a.shape[0]
    n_e = w_router.shape[2]
    ch = w_dw.shape[2]
    moba_w = w_in_b.shape[2] - MEM_W
    n_heads = moba_w // HEAD_DIM
    n_mem = mem_prompt.shape[1]
    n_pool, page = cache_k.shape[0], cache_k.shape[1]
    n_pages = page_table.shape[1]
    past_len = n_pages * page
    alpha = (2 * depth) ** 0.25
    tl = ROW_TILE
    assert dseq == 1 and db % SUBLANES == 0 and db <= tl and seq_len % tl == 0
    assert ch + MEM_W == d and moba_w == ch and MOBA_BLOCK % page == 0 and past_len % MOBA_BLOCK == 0
    n_prompt = n_seq * seq_len
    n_tok = n_prompt + db
    t_pad = n_prompt + tl
    ppb = MOBA_BLOCK // page
    n_items = n_e + (n_tok * TOP_K) // MOE_CAP

    x = jnp.concatenate([x_prompt.reshape(n_prompt, d), x_sample.reshape(db, d),
                         jnp.zeros((tl - db, d), F32)], axis=0)

    pos = jnp.concatenate([jnp.tile(jnp.arange(seq_len, dtype=jnp.int32), n_seq),
                           jnp.full((db,), past_len, jnp.int32), jnp.zeros((tl - db,), jnp.int32)])
    half = HEAD_DIM // 2
    inv = ROPE_THETA ** (-jnp.arange(half, dtype=F32) / half)
    ang = pos.astype(F32)[:, None] * inv[None, :]
    rope = (jnp.concatenate([jnp.cos(ang), jnp.cos(ang)], axis=1),
            jnp.concatenate([-jnp.sin(ang), jnp.sin(ang)], axis=1))

    mem2 = mem_prompt.reshape(n_seq * n_mem, d)
    mkv = [_dense(mem2, w_mem_kv[l].astype(BF16), n_out=2, name="mem_kv") for l in range(depth)]
    new_mem_k = jnp.stack([kv[0] for kv in mkv]).reshape(depth, n_seq, n_mem, MEM_W)
    new_mem_v = jnp.stack([kv[1] for kv in mkv]).reshape(depth, n_seq, n_mem, MEM_W)
    mem_k_s = cache_mem_k.reshape(depth, db, n_mem, MEM_W)
    mem_v_s = cache_mem_v.reshape(depth, db, n_mem, MEM_W)

    pt_flat = page_table.reshape(-1).astype(jnp.int32)
    cache_k_hm = jnp.swapaxes(cache_k, 1, 2)
    cache_v_hm = jnp.swapaxes(cache_v, 1, 2)

    conv_tails, conv_new = [], []
    kf = vf = kmean_s = None
    for l in range(depth):
        if l < n_a:
            h = _dense(x, w_in_a[l].astype(BF16), name="in_proj_a")
            c_act, tail, us = _conv_module(h, jnp.swapaxes(cache_conv[l], 0, 1), w_dw[l], b_dw[l],
                                           conv_ln_g[l], conv_ln_b[l], n_seq=n_seq, seq_len=seq_len, ch=ch)
            conv_tails.append(tail[:, CONV_HALO - (CONV_W - 1):])
            conv_new.append(jnp.concatenate([cache_conv[l][:, 1:], us[:, None, :]], axis=1))
            m_act = _mem_attention(h, 2 * ch // MEM_W, new_mem_k[l], new_mem_v[l], mem_k_s[l], mem_v_s[l],
                                   n_seq=n_seq, seq_len=seq_len)
            a1, a1s, w_out = c_act, None, w_out_a[l]
        else:
            jb = l - n_a
            if kf is None:
                kf, vf = _dense(x, w_kv_shared.astype(BF16), rope=rope, n_rope_heads=n_heads, n_out=2,
                                name="kv_proj")
                kmean_s = _page_means(cache_k_hm, pt_flat, db=db, n_pages=n_pages)
            hq = _dense(x, w_in_b[jb].astype(BF16), rope=rope, n_rope_heads=n_heads, name="in_proj_b")
            o_prompt = _moba_prompt(hq, kf, vf, n_seq=n_seq, seq_len=seq_len, n_heads=n_heads)
            sel = _moba_select(hq, kmean_s, row0=n_prompt, db=db, n_heads=n_heads)[:, :, :MOBA_TOPK]
            blk_pages = sel[..., None] * ppb + jnp.arange(ppb, dtype=jnp.int32)
            pages = jnp.take_along_axis(page_table.astype(jnp.int32)[:, None, :],
                                        blk_pages.reshape(db, n_heads, MOBA_TOPK * ppb), axis=2)
            four = lambda a: a[n_prompt:n_tok].reshape(db, n_heads, 1, HEAD_DIM)
            o_s = _moba_decode(pages.reshape(-1), four(hq[:, :moba_w]), four(kf), four(vf), cache_k_hm, cache_v_hm,
                               db=db, n_heads=n_heads, n_sel=MOBA_TOPK * ppb)
            a1s = jnp.concatenate([o_s.reshape(db, moba_w), jnp.zeros((tl - db, moba_w), F32)], axis=0).astype(BF16)
            m_act = _mem_attention(hq, moba_w // MEM_W, new_mem_k[l], new_mem_v[l], mem_k_s[l], mem_v_s[l],
                                   n_seq=n_seq, seq_len=seq_len)
            a1, w_out = o_prompt, w_out_b[jb]
        x, xp = _outproj_ln(a1, a1s, m_act, w_out.astype(BF16), x, ln1_g[l], ln1_b[l], alpha=alpha)

        top_e, gates = _router(x, w_router[l].T, b_router[l])
        plan = _route_plan(top_e, n_tok, n_e, n_items, t_pad)
        yr = _moe_experts(xp, plan, w_gu, b_gu, w_down, b_down, l, n_items)
        x = _combine_ln(yr, plan[4], gates.T, x, ln2_g[l], ln2_b[l], alpha=alpha)

    y_prompt = x[:n_prompt].reshape(n_seq, seq_len, d)
    y_sample = x[n_prompt:n_tok].reshape(db, 1, d)
    heads4 = lambda a, n, s: a.reshape(n, s, n_heads, HEAD_DIM)
    mem5 = lambda a: a.reshape(depth, n_seq, n_mem, N_MEM_HEADS, HEAD_DIM)
    return (y_prompt, y_sample, jnp.stack(conv_tails), jnp.stack(conv_new),
            heads4(kf[:n_prompt], n_seq, seq_len), heads4(vf[:n_prompt], n_seq, seq_len),
            heads4(kf[n_prompt:n_tok], db, 1), heads4(vf[n_prompt:n_tok], db, 1),
            mem5(new_mem_k), mem5(new_mem_v))
```
